```python
import math
import jax
import jax.numpy as jnp
from jax import lax
import numpy as np

D_MODEL = 1024
BATCH = 16
SEQ = 4096
DEPTH = 1
DEC_BATCH = 16
DEC_SEQ = 32
PAST_LEN = 2048

CHUNK = 64
EPS = 1e-6
NEG_INF = -1e30
Q_BLOCK = 128

A_HEADS = 8
A_DK = 64
A_DV = 2 * A_DK

S_INNER = 2 * D_MODEL
S_HEAD_P = 64
S_HEADS = S_INNER // S_HEAD_P
S_GROUPS = 4
S_HPG = S_HEADS // S_GROUPS
S_STATE = 128
S_CONV = 4
S_CONV_DIM = S_INNER + 2 * S_GROUPS * S_STATE

P_HEADS = 8
P_NKEYS = 128
P_EXPERTS = P_NKEYS * P_NKEYS
P_DKEY = 128
P_TOPK = 16
P_BLOCK = 256

PLE_DIM = 256

C_Q = A_HEADS * 2 * A_DK
C_K = A_HEADS * 2 * A_DK
C_V = A_HEADS * A_DV
C_Z = S_INNER
C_XBC = S_CONV_DIM
C_DT = S_HEADS
C_GATE = 2 * D_MODEL
IN_COLS = C_Q + C_K + C_V + C_Z + C_XBC + C_DT + C_GATE
SPLIT_AT = (C_Q, C_Q + C_K, C_Q + C_K + C_V, C_Q + C_K + C_V + C_Z,
            C_Q + C_K + C_V + C_Z + C_XBC, C_Q + C_K + C_V + C_Z + C_XBC + C_DT)

kernel_name = 'hybrid_diffattn_ssd_peer_stream_step'


def _rms(xf):
    return xf * lax.rsqrt(jnp.mean(xf * xf, axis=-1, keepdims=True) + EPS)


def rms_norm(x, g):
    return (_rms(x.astype(jnp.float32)) * g.astype(jnp.float32)).astype(x.dtype)


def alibi_slopes():
    return 2.0 ** (-8.0 * (jnp.arange(A_HEADS, dtype=jnp.float32) + 1.0) / A_HEADS)


def diff_scores_apply(q, k, v, q_pos, k_pos, lam):
    s = jnp.einsum('bqhjd,bkhjd->bhjqk', q, k).astype(jnp.float32) * (A_DK ** -0.5)
    dist = jnp.abs(q_pos[:, None] - k_pos[None, :]).astype(jnp.float32)
    bias = -alibi_slopes()[:, None, None, None] * dist
    mask = (k_pos[None, :] // CHUNK) <= (q_pos[:, None] // CHUNK)
    a = jax.nn.softmax(jnp.where(mask, s + bias, NEG_INF), axis=-1)
    w = a[:, :, 0] - lam * a[:, :, 1]
    return jnp.einsum('bhqk,bkhd->bqhd', w.astype(v.dtype), v)


def diff_attention(q, k_all, v_all, q_pos, k_pos, lam, lam_init, sub_g, w_o):
    bsz, lq = q.shape[:2]
    if lq >= Q_BLOCK and lq % Q_BLOCK == 0:
        nb = lq // Q_BLOCK
        qb = jnp.swapaxes(q.reshape(bsz, nb, Q_BLOCK, A_HEADS, 2, A_DK), 0, 1)
        pb = q_pos.reshape(nb, Q_BLOCK)
        ob = lax.map(lambda t: diff_scores_apply(t[0], k_all, v_all, t[1], k_pos, lam), (qb, pb))
        o = jnp.swapaxes(ob, 0, 1).reshape(bsz, lq, A_HEADS, A_DV)
    else:
        o = diff_scores_apply(q, k_all, v_all, q_pos, k_pos, lam)
    o = rms_norm(o, sub_g) * (1.0 - lam_init)
    return o.reshape(bsz, lq, A_HEADS * A_DV) @ w_o


def causal_conv(xbc, past, w, b):
    xp = jnp.concatenate([past.astype(xbc.dtype), xbc], axis=1)
    y = lax.conv_general_dilated(xp, w[:, None, :].astype(xbc.dtype), window_strides=(1,),
                                 padding='VALID', dimension_numbers=('NWC', 'WIO', 'NWC'),
                                 feature_group_count=S_CONV_DIM)
    return jax.nn.silu(y + b.astype(xbc.dtype)), xp[:, -(S_CONV - 1):]


def ssd(x, dt, a, bm, cm, h0, q_len):
    bsz, length = x.shape[:2]
    nc = length // q_len
    xc = jnp.moveaxis(x.reshape(bsz, nc, q_len, S_GROUPS, S_HPG, S_HEAD_P), 1, 0)
    dtc = dt.reshape(bsz, nc, q_len, S_GROUPS, S_HPG)
    dac = jnp.moveaxis(dtc * a.reshape(S_GROUPS, S_HPG), 1, 0)
    dtc = jnp.moveaxis(dtc, 1, 0)
    bc = jnp.moveaxis(bm.reshape(bsz, nc, q_len, S_GROUPS, S_STATE), 1, 0)
    cc = jnp.moveaxis(cm.reshape(bsz, nc, q_len, S_GROUPS, S_STATE), 1, 0)
    causal = jnp.tril(jnp.ones((q_len, q_len), dtype=bool))[None, :, :, None, None]

    def step(h, inp):
        xk, dtk, dak, bk, ck = inp
        acum = jnp.cumsum(dak, axis=1)
        seg = acum[:, :, None] - acum[:, None, :]
        lmat = jnp.exp(jnp.where(causal, seg, -jnp.inf))
        cb = jnp.einsum('blgn,bsgn->blsg', ck, bk)
        m = cb[..., None] * lmat * dtk[:, None]
        y = jnp.einsum('blsgh,bsghp->blghp', m, xk)
        y = y + jnp.einsum('blgn,bghpn->blghp', ck, h) * jnp.exp(acum)[..., None]
        wdec = jnp.exp(acum[:, -1:] - acum) * dtk
        h_new = jnp.exp(acum[:, -1])[..., None, None] * h + jnp.einsum(
            'bsgn,bsghp->bghpn', bk, wdec[..., None] * xk)
        return h_new, y

    h_init = h0.reshape(bsz, S_GROUPS, S_HPG, S_HEAD_P, S_STATE)
    h_fin, ys = lax.scan(step, h_init, (xc, dtc, dac, bc, cc))
    y = jnp.moveaxis(ys, 0, 1).reshape(bsz, length, S_HEADS, S_HEAD_P)
    return y, h_fin.reshape(bsz, S_HEADS, S_HEAD_P, S_STATE)


def ssm_branch(z, xbc, dt_raw, conv_past, h0, lp, q_len):
    bsz, length = z.shape[:2]
    xbc_act, conv_new = causal_conv(xbc, conv_past, lp['conv_w'], lp['conv_b'])
    xs = xbc_act[..., :S_INNER].reshape(bsz, length, S_HEADS, S_HEAD_P).astype(jnp.float32)
    bm = xbc_act[..., S_INNER:S_INNER + S_GROUPS * S_STATE].reshape(
        bsz, length, S_GROUPS, S_STATE).astype(jnp.float32)
    cm = xbc_act[..., S_INNER + S_GROUPS * S_STATE:].reshape(
        bsz, length, S_GROUPS, S_STATE).astype(jnp.float32)
    dt = jax.nn.softplus(dt_raw.astype(jnp.float32) + lp['dt_bias'].astype(jnp.float32))
    a = -jnp.exp(lp['a_log'].astype(jnp.float32))
    y, h_fin = ssd(xs, dt, a, bm, cm, h0.astype(jnp.float32), q_len)
    y = y + lp['d_skip'].astype(jnp.float32)[:, None] * xs
    yg = y.reshape(bsz, length, S_INNER) * jax.nn.silu(z.astype(jnp.float32))
    yg = _rms(yg.reshape(bsz, length, S_GROUPS, S_INNER // S_GROUPS)).reshape(bsz, length, S_INNER)
    yg = (yg * lp['ssm_norm_g'].astype(jnp.float32)).astype(z.dtype)
    return yg @ lp['w_ssm_o'], conv_new, h_fin.astype(z.dtype)


def peer_ffn(h, wq, k1, k2, u, v):
    bsz, length, d = h.shape
    t = h.reshape(-1, d)
    n_tok = t.shape[0]
    nb = -(-n_tok // P_BLOCK)
    t = jnp.pad(t, ((0, nb * P_BLOCK - n_tok), (0, 0))).reshape(nb, P_BLOCK, d)
    half = P_DKEY // 2

    def block(tb):
        q = (tb @ wq).reshape(P_BLOCK, P_HEADS, P_DKEY)
        s1 = jnp.einsum('thd,nd->thn', q[..., :half], k1).astype(jnp.float32)
        s2 = jnp.einsum('thd,nd->thn', q[..., half:], k2).astype(jnp.float32)
        v1, i1 = lax.top_k(s1, P_TOPK)
        v2, i2 = lax.top_k(s2, P_TOPK)
        cand = (v1[..., :, None] + v2[..., None, :]).reshape(P_BLOCK, P_HEADS, P_TOPK * P_TOPK)
        cidx = (i1[..., :, None] * P_NKEYS + i2[..., None, :]).reshape(P_BLOCK, P_HEADS, P_TOPK * P_TOPK)
        sc, pos = lax.top_k(cand, P_TOPK)
        idx = jnp.take_along_axis(cidx, pos, axis=-1)
        g = jax.nn.softmax(sc, axis=-1)
        pre = jnp.einsum('td,thkd->thk', tb, u[idx])
        act = (jax.nn.gelu(pre.astype(jnp.float32)) * g).astype(tb.dtype)
        return jnp.einsum('thk,thkd->td', act, v[idx])

    out = lax.map(block, t).reshape(nb * P_BLOCK, d)[:n_tok]
    return out.reshape(bsz, length, d)


def hybrid_layer(x, p_i, k_past, v_past, conv_past, ssm_past, q_pos, k_pos, ssd_len, lp, lam_init):
    bsz, length = x.shape[:2]
    h = rms_norm(x, lp['norm_mix_g'])
    q, k, v, z, xbc, dt_raw, gates = jnp.split(h @ lp['w_in'], SPLIT_AT, axis=-1)
    q = q.reshape(bsz, length, A_HEADS, 2, A_DK)
    k = k.reshape(bsz, length, A_HEADS, 2, A_DK)
    v = v.reshape(bsz, length, A_HEADS, A_DV)
    k_all = k if k_past is None else jnp.concatenate([k_past.astype(k.dtype), k], axis=1)
    v_all = v if v_past is None else jnp.concatenate([v_past.astype(v.dtype), v], axis=1)
    f32 = jnp.float32
    lam = (jnp.exp(jnp.sum(lp['lam_q1'].astype(f32) * lp['lam_k1'].astype(f32)))
           - jnp.exp(jnp.sum(lp['lam_q2'].astype(f32) * lp['lam_k2'].astype(f32))) + lam_init)
    o_a = diff_attention(q, k_all, v_all, q_pos, k_pos, lam, lam_init, lp['attn_subln_g'], lp['w_attn_o'])
    if conv_past is None:
        conv_past = jnp.zeros((bsz, S_CONV - 1, S_CONV_DIM), x.dtype)
    if ssm_past is None:
        ssm_past = jnp.zeros((bsz, S_HEADS, S_HEAD_P, S_STATE), jnp.float32)
    o_s, conv_new, ssm_new = ssm_branch(z, xbc, dt_raw, conv_past, ssm_past, lp, ssd_len)
    g_a, g_s = jnp.split(gates, 2, axis=-1)
    x = x + (jax.nn.sigmoid(g_a) * o_a + jax.nn.sigmoid(g_s) * o_s) @ lp['w_out']
    x = x + peer_ffn(rms_norm(x, lp['norm_ffn_g']), lp['peer_wq'], lp['peer_k1'], lp['peer_k2'],
                     lp['peer_u'], lp['peer_v'])
    x = x + jax.nn.sigmoid(rms_norm(x, lp['norm_ple_g']) @ lp['ple_w_gate']) * (p_i @ lp['ple_w_proj'])
    return x, k, v, conv_new, ssm_new


def setup_inputs(seed: int = 0) -> dict:
    key = jax.random.key(seed)
    ks = iter(jax.random.split(key, 40))
    f32 = jnp.float32

    def nrm(shape, scale):
        return jax.random.normal(next(ks), shape, f32) * scale

    def gain(shape):
        return 1.0 + 0.02 * jax.random.normal(next(ks), shape, f32)

    dt0 = jnp.exp(jax.random.uniform(next(ks), (DEPTH, S_HEADS), f32)
                  * (math.log(0.1) - math.log(0.001)) + math.log(0.001))
    return {
        'x_prompt': nrm((BATCH, SEQ, D_MODEL), 1.0),
        'x_sample': nrm((DEC_BATCH, DEC_SEQ, D_MODEL), 1.0),
        'p_prompt': nrm((DEPTH, BATCH, SEQ, PLE_DIM), 1.0),
        'p_sample': nrm((DEPTH, DEC_BATCH, DEC_SEQ, PLE_DIM), 1.0),
        'cache_k': nrm((DEPTH, DEC_BATCH, PAST_LEN, A_HEADS, 2, A_DK), 1.0),
        'cache_v': nrm((DEPTH, DEC_BATCH, PAST_LEN, A_HEADS, A_DV), 1.0),
        'state_conv': nrm((DEPTH, DEC_BATCH, S_CONV - 1, S_CONV_DIM), 1.0),
        'state_ssm': nrm((DEPTH, DEC_BATCH, S_HEADS, S_HEAD_P, S_STATE), 0.5),
        'norm_mix_g': gain((DEPTH, D_MODEL)),
        'w_in': nrm((DEPTH, D_MODEL, IN_COLS), D_MODEL ** -0.5),
        'lam_q1': nrm((DEPTH, A_DK), 0.1),
        'lam_k1': nrm((DEPTH, A_DK), 0.1),
        'lam_q2': nrm((DEPTH, A_DK), 0.1),
        'lam_k2': nrm((DEPTH, A_DK), 0.1),
        'attn_subln_g': gain((DEPTH, A_DV)),
        'w_attn_o': nrm((DEPTH, A_HEADS * A_DV, D_MODEL), (A_HEADS * A_DV) ** -0.5),
        'conv_w': nrm((DEPTH, S_CONV, S_CONV_DIM), S_CONV ** -0.5),
        'conv_b': nrm((DEPTH, S_CONV_DIM), 0.02),
        'dt_bias': dt0 + jnp.log(-jnp.expm1(-dt0)),
        'a_log': jnp.log(jax.random.uniform(next(ks), (DEPTH, S_HEADS), f32, 1.0, 16.0)),
        'd_skip': 1.0 + 0.1 * jax.random.normal(next(ks), (DEPTH, S_HEADS), f32),
        'ssm_norm_g': gain((DEPTH, S_INNER)),
        'w_ssm_o': nrm((DEPTH, S_INNER, D_MODEL), S_INNER ** -0.5),
        'w_out': nrm((DEPTH, D_MODEL, D_MODEL), D_MODEL ** -0.5),
        'norm_ffn_g': gain((DEPTH, D_MODEL)),
        'peer_wq': nrm((DEPTH, D_MODEL, P_HEADS * P_DKEY), D_MODEL ** -0.5),
        'peer_k1': nrm((DEPTH, P_NKEYS, P_DKEY // 2), (P_DKEY // 2) ** -0.5),
        'peer_k2': nrm((DEPTH, P_NKEYS, P_DKEY // 2), (P_DKEY // 2) ** -0.5),
        'peer_u': nrm((DEPTH, P_EXPERTS, D_MODEL), D_MODEL ** -0.5),
        'peer_v': nrm((DEPTH, P_EXPERTS, D_MODEL), 0.3),
        'norm_ple_g': gain((DEPTH, D_MODEL)),
        'ple_w_gate': nrm((DEPTH, D_MODEL, D_MODEL), D_MODEL ** -0.5),
        'ple_w_proj': nrm((DEPTH, PLE_DIM, D_MODEL), PLE_DIM ** -0.5),
        'final_norm_g': gain((D_MODEL,)),
    }


def reference(x_prompt, x_sample, p_prompt, p_sample, cache_k, cache_v, state_conv, state_ssm,
              norm_mix_g, w_in, lam_q1, lam_k1, lam_q2, lam_k2, attn_subln_g, w_attn_o,
              conv_w, conv_b, dt_bias, a_log, d_skip, ssm_norm_g, w_ssm_o, w_out,
              norm_ffn_g, peer_wq, peer_k1, peer_k2, peer_u, peer_v,
              norm_ple_g, ple_w_gate, ple_w_proj, final_norm_g):

    def run_group(x, p, ck, cv, cconv, cssm, q_pos, k_pos, ssd_len):
        k_rows, v_rows, conv_rows, ssm_rows = [], [], [], []
        for i in range(DEPTH):
            lp = {
                'norm_mix_g': norm_mix_g[i], 'w_in': w_in[i],
                'lam_q1': lam_q1[i], 'lam_k1': lam_k1[i], 'lam_q2': lam_q2[i], 'lam_k2': lam_k2[i],
                'attn_subln_g': attn_subln_g[i], 'w_attn_o': w_attn_o[i],
                'conv_w': conv_w[i], 'conv_b': conv_b[i], 'dt_bias': dt_bias[i], 'a_log': a_log[i],
                'd_skip': d_skip[i], 'ssm_norm_g': ssm_norm_g[i], 'w_ssm_o': w_ssm_o[i],
                'w_out': w_out[i], 'norm_ffn_g': norm_ffn_g[i], 'peer_wq': peer_wq[i],
                'peer_k1': peer_k1[i], 'peer_k2': peer_k2[i], 'peer_u': peer_u[i], 'peer_v': peer_v[i],
                'norm_ple_g': norm_ple_g[i], 'ple_w_gate': ple_w_gate[i], 'ple_w_proj': ple_w_proj[i],
            }
            lam_init = 0.8 - 0.6 * math.exp(-0.3 * i)
            x, k_new, v_new, conv_new, ssm_new = hybrid_layer(
                x, p[i],
                None if ck is None else ck[i], None if cv is None else cv[i],
                None if cconv is None else cconv[i], None if cssm is None else cssm[i],
                q_pos, k_pos, ssd_len, lp, lam_init)
            k_rows.append(k_new)
            v_rows.append(v_new)
            conv_rows.append(conv_new)
            ssm_rows.append(ssm_new)
        return (rms_norm(x, final_norm_g), jnp.stack(k_rows), jnp.stack(v_rows),
                jnp.stack(conv_rows), jnp.stack(ssm_rows))

    lp_len = x_prompt.shape[1]
    pos_p = jnp.arange(lp_len, dtype=jnp.int32)
    y_prompt, k_prompt, v_prompt, conv_prompt, ssm_prompt = run_group(
        x_prompt, p_prompt, None, None, None, None, pos_p, pos_p, CHUNK)

    ls_len = x_sample.shape[1]
    past = cache_k.shape[2]
    q_pos_s = past + jnp.arange(ls_len, dtype=jnp.int32)
    k_pos_s = jnp.arange(past + ls_len, dtype=jnp.int32)
    y_sample, k_sample, v_sample, conv_sample, ssm_sample = run_group(
        x_sample, p_sample, cache_k, cache_v, state_conv, state_ssm, q_pos_s, k_pos_s, ls_len)

    return (y_prompt, y_sample, k_prompt, v_prompt, conv_prompt, ssm_prompt,
            k_sample, v_sample, conv_sample, ssm_sample)
```

```python
import functools
import math

import numpy as np
import jax
import jax.numpy as jnp
from jax import lax
from jax.experimental import pallas as pl
from jax.experimental.pallas import tpu as pltpu

F32 = jnp.float32
BF16 = jnp.bfloat16

EPS = 1e-6
CHUNK = 64
NEG_INF = -1e30
P_TOPK = 16

LANES = 128
SUBLANES = 8
VMEM_LIMIT = 52 * 1024 * 1024

NT_DIMS = (((1,), (1,)), ((), ()))
TN_DIMS = (((0,), (0,)), ((), ()))


def _cparams(*sem):
    return pltpu.CompilerParams(dimension_semantics=sem, vmem_limit_bytes=VMEM_LIMIT)


def _rms(x):
    return x * lax.rsqrt(jnp.mean(x * x, axis=-1, keepdims=True) + EPS)


def _split3(x):
    hi = x.astype(BF16)
    r1 = x - hi.astype(F32)
    mid = r1.astype(BF16)
    lo = (r1 - mid.astype(F32)).astype(BF16)
    return hi, mid, lo


def _dot3(x, m, dims=None):
    out = None
    for piece in _split3(x):
        if dims is None:
            t = jnp.dot(piece, m, preferred_element_type=F32)
        else:
            t = lax.dot_general(piece, m, dims, preferred_element_type=F32)
        out = t if out is None else out + t
    return out


def _rms_matmul_kernel(x_ref, g_ref, w_ref, *o_refs):
    h = (_rms(x_ref[...]) * g_ref[...]).astype(BF16)
    y = jnp.dot(h, w_ref[...], preferred_element_type=F32)
    for o_ref in o_refs:
        o_ref[...] = y.astype(o_ref.dtype)


def rms_matmul(x, g, w, out_dtypes, tm=512):
    n, d = x.shape
    c = w.shape[1]
    tm = min(tm, n)
    tn = min(c, 1024)
    assert n % tm == 0 and c % tn == 0
    outs = pl.pallas_call(
        _rms_matmul_kernel,
        grid=(c // tn, n // tm),
        in_specs=[pl.BlockSpec((tm, d), lambda j, i: (i, 0)),
                  pl.BlockSpec((1, d), lambda j, i: (0, 0)),
                  pl.BlockSpec((d, tn), lambda j, i: (0, j))],
        out_specs=[pl.BlockSpec((tm, tn), lambda j, i: (i, j)) for _ in out_dtypes],
        out_shape=[jax.ShapeDtypeStruct((n, c), dt) for dt in out_dtypes],
        compiler_params=_cparams("parallel", "parallel"),
        name="rms_matmul",
    )(x, g.reshape(1, d), w)
    return outs


def _attn_kernel(qi_tab, ki_tab, par_ref, q_ref, k_ref, v_ref, g_ref, o_ref,
                 m_scr, l_scr, acc_scr, *, tq, tk, rq, q_off, lk, n_heads, out_scale):
    h = pl.program_id(1)
    step = pl.program_id(2)
    qi = qi_tab[step]
    ki = ki_tab[step]
    slope = par_ref[h]
    lam = par_ref[n_heads]
    dk = LANES // 2

    qs = q_off + qi * tq
    ks = ki * tk
    last_vis = (((qs + tq - 1) // CHUNK) * CHUNK + CHUNK - 1) // tk
    k_last = jnp.minimum((lk - 1) // tk, last_vis)

    @pl.when(ki == 0)
    def _():
        m_scr[...] = jnp.full(m_scr.shape, NEG_INF, F32)
        l_scr[...] = jnp.zeros(l_scr.shape, F32)
        acc_scr[...] = jnp.zeros(acc_scr.shape, F32)

    k = k_ref[0]
    v = v_ref[0]
    lane = lax.broadcasted_iota(jnp.int32, (rq, LANES), 1)

    def process(r, bias_fn):
        rows = pl.ds(pl.multiple_of(r * rq, rq), rq)
        q = q_ref[0, rows, :]
        zero = jnp.zeros_like(q)
        qq = jnp.concatenate([jnp.where(lane < dk, q, zero), jnp.where(lane >= dk, q, zero)], axis=0)
        s = lax.dot_general(qq, k, NT_DIMS, preferred_element_type=F32)
        bm = bias_fn(r)
        ps = []
        for j in range(2):
            sj = s[j * rq:(j + 1) * rq] + bm
            m_old = m_scr[j, rows, :]
            m_new = jnp.maximum(m_old, jnp.max(sj, axis=-1, keepdims=True))
            alpha = jnp.exp(m_old - m_new)
            p = jnp.exp(sj - m_new)
            l_scr[j, rows, :] = alpha * l_scr[j, rows, :] + jnp.sum(p, axis=-1, keepdims=True)
            m_scr[j, rows, :] = m_new
            acc_scr[j, rows, :] = alpha * acc_scr[j, rows, :]
            ps.append(p.astype(BF16))
        pv = jnp.dot(jnp.concatenate(ps, axis=0), v, preferred_element_type=F32)
        for j in range(2):
            acc_scr[j, rows, :] = acc_scr[j, rows, :] + pv[j * rq:(j + 1) * rq]

    def bias_row(r):
        kpos = ks + lax.broadcasted_iota(jnp.int32, (1, tk), 1)
        return slope * (kpos - qs).astype(F32)

    def bias_full(r):
        ii = r * rq + lax.broadcasted_iota(jnp.int32, (rq, tk), 0)
        qpos = qs + ii
        kpos = ks + lax.broadcasted_iota(jnp.int32, (rq, tk), 1)
        dist = jnp.abs(qpos - kpos)
        visible = (lax.shift_right_logical(kpos, 6) <= lax.shift_right_logical(qpos, 6)) & (kpos < lk)
        return jnp.where(visible, slope * (ii - dist).astype(F32), NEG_INF)

    n_sub = tq // rq
    simple = (ks + tk - 1 <= qs) & (ks + tk <= lk)

    @pl.when(simple)
    def _():
        lax.fori_loop(0, n_sub, lambda r, c: (process(r, bias_row), c)[1], 0)

    @pl.when(jnp.logical_not(simple) & (ki <= k_last))
    def _():
        lax.fori_loop(0, n_sub, lambda r, c: (process(r, bias_full), c)[1], 0)

    @pl.when(ki == k_last)
    def _():
        o = acc_scr[0] / l_scr[0] - lam * (acc_scr[1] / l_scr[1])
        o = _rms(o) * g_ref[...] * out_scale
        o_ref[0] = o.astype(o_ref.dtype)


def diff_attention(q, k, v, par, sub_g, *, lq, lk, q_off, tq, tk, rq, n_heads, out_scale):
    bsz = q.shape[0]
    lk_pad = k.shape[1]
    assert lq % tq == 0 and lk_pad % tk == 0 and tq % rq == 0 and CHUNK == 64
    nq, nk = lq // tq, lk_pad // tk
    pairs = []
    for qi in range(nq):
        qs = q_off + qi * tq
        k_last = min((lk - 1) // tk, (((qs + tq - 1) // CHUNK) * CHUNK + CHUNK - 1) // tk)
        pairs += [(qi, ki) for ki in range(k_last + 1)]
    qi_tab = jnp.asarray(np.array([p[0] for p in pairs], np.int32))
    ki_tab = jnp.asarray(np.array([p[1] for p in pairs], np.int32))
    kern = functools.partial(_attn_kernel, tq=tq, tk=tk, rq=rq, q_off=q_off, lk=lk,
                             n_heads=n_heads, out_scale=out_scale)
    grid_spec = pltpu.PrefetchScalarGridSpec(
        num_scalar_prefetch=2,
        grid=(bsz, n_heads, len(pairs)),
        in_specs=[pl.BlockSpec(memory_space=pltpu.SMEM),
                  pl.BlockSpec((1, tq, LANES), lambda b, h, s, qt, kt: (b, qt[s], h)),
                  pl.BlockSpec((1, tk, LANES), lambda b, h, s, qt, kt: (b, kt[s], h)),
                  pl.BlockSpec((1, tk, LANES), lambda b, h, s, qt, kt: (b, kt[s], h)),
                  pl.BlockSpec((1, LANES), lambda b, h, s, qt, kt: (0, 0))],
        out_specs=pl.BlockSpec((1, tq, LANES), lambda b, h, s, qt, kt: (b, qt[s], h)),
        scratch_shapes=[pltpu.VMEM((2, tq, 1), F32), pltpu.VMEM((2, tq, 1), F32),
                        pltpu.VMEM((2, tq, LANES), F32)])
    return pl.pallas_call(
        kern, grid_spec=grid_spec,
        out_shape=jax.ShapeDtypeStruct((bsz, lq, n_heads * LANES), BF16),
        compiler_params=_cparams("parallel", "parallel", "arbitrary"),
        name="diff_attention",
    )(qi_tab, ki_tab, par, q, k, v, sub_g.reshape(1, LANES))


def _ssm_kernel(z_ref, x_ref, b_ref, c_ref, dt_ref, dtb_ref, alog_ref, dsk_ref,
                cwx_ref, cwb_ref, cwc_ref, cbx_ref, cbb_ref, cbc_ref,
                px_ref, pb_ref, pc_ref, h0_ref, ng_ref,
                yg_ref, hfin_ref, ht_scr, xpx, xpb, xpc, *, lc, l_valid, n_conv):
    c = pl.program_id(2)
    nc = pl.num_programs(2)
    npair = ht_scr.shape[0]
    pad = SUBLANES
    nh = n_conv - 1

    @pl.when(c == 0)
    def _():
        for pr in range(npair):
            ht_scr[pr] = h0_ref[0, pr].T
        xpx[pad - nh:pad, :] = px_ref[0]
        xpb[pad - nh:pad, :] = pb_ref[0]
        xpc[pad - nh:pad, :] = pc_ref[0]

    @pl.when(c > 0)
    def _():
        xpx[pad - nh:pad, :] = xpx[pad + lc - nh:pad + lc, :]
        xpb[pad - nh:pad, :] = xpb[pad + lc - nh:pad + lc, :]
        xpc[pad - nh:pad, :] = xpc[pad + lc - nh:pad + lc, :]

    xpx[pad:pad + lc, :] = x_ref[0]
    xpb[pad:pad + lc, :] = b_ref[0]
    xpc[pad:pad + lc, :] = c_ref[0]

    def conv_silu(xp, cw_ref, cb_ref):
        acc = cb_ref[...]
        for w in range(n_conv):
            acc = acc + cw_ref[w:w + 1, :] * xp[pad - nh + w:pad - nh + w + lc, :]
        return acc * jax.nn.sigmoid(acc)

    xs = conv_silu(xpx, cwx_ref, cbx_ref)
    bm = conv_silu(xpb, cwb_ref, cbb_ref).astype(BF16)
    cm = conv_silu(xpc, cwc_ref, cbc_ref).astype(BF16)

    t_col = c * lc + lax.broadcasted_iota(jnp.int32, (lc, LANES), 0)
    dt_c = jax.nn.softplus(dt_ref[0] + dtb_ref[0])
    dt_c = jnp.where(t_col < l_valid, dt_c, 0.0)
    da_c = dt_c * (-jnp.exp(alog_ref[0]))
    ri = lax.broadcasted_iota(jnp.int32, (lc, lc), 0)
    ci = lax.broadcasted_iota(jnp.int32, (lc, lc), 1)
    causal = ri >= ci
    tril = jnp.where(causal, 1.0, 0.0).astype(BF16)
    acum_c = None
    for piece in _split3(da_c):
        t = jnp.dot(tril, piece, preferred_element_type=F32)
        acum_c = t if acum_c is None else acum_c + t
    acum_r = acum_c.T
    dt_r = dt_c.T
    a_last = acum_c[lc - 1:lc, :]
    wdec_c = jnp.exp(a_last - acum_c) * dt_c
    eac_c = jnp.exp(acum_c)
    ea_last = jnp.exp(a_last)

    cb = lax.dot_general(cm, bm, NT_DIMS, preferred_element_type=F32)
    lane = lax.broadcasted_iota(jnp.int32, (lc, LANES), 1)
    lane1 = lax.broadcasted_iota(jnp.int32, (1, LANES), 1)
    half = LANES // 2
    ys = []
    for pr in range(npair):
        xp_f = xs[:, pr * LANES:(pr + 1) * LANES]
        y = dsk_ref[:, pr * LANES:(pr + 1) * LANES] * xp_f
        for e in range(2):
            hh = 2 * pr + e
            seg = acum_c[:, hh:hh + 1] - acum_r[hh:hh + 1, :]
            lmat = jnp.exp(jnp.where(causal, seg, NEG_INF))
            m = (cb * lmat * dt_r[hh:hh + 1, :]).astype(BF16)
            sel = (lane < half) if e == 0 else (lane >= half)
            xh = jnp.where(sel, xp_f, 0.0).astype(BF16)
            y = y + jnp.dot(m, xh, preferred_element_type=F32)
        h_a, h_b = 2 * pr, 2 * pr + 1
        scale = jnp.where(lane < half, eac_c[:, h_a:h_a + 1], eac_c[:, h_b:h_b + 1])
        ht = ht_scr[pr]
        y = y + jnp.dot(cm, ht.astype(BF16), preferred_element_type=F32) * scale
        wd = jnp.where(lane < half, wdec_c[:, h_a:h_a + 1], wdec_c[:, h_b:h_b + 1])
        dec = jnp.where(lane1 < half, ea_last[:, h_a:h_a + 1], ea_last[:, h_b:h_b + 1])
        upd = lax.dot_general(bm, (wd * xp_f).astype(BF16), TN_DIMS, preferred_element_type=F32)
        ht_scr[pr] = dec * ht + upd
        ys.append(y)
    y = jnp.concatenate(ys, axis=-1)

    zz = z_ref[0]
    yg = y * (zz * jax.nn.sigmoid(zz))
    yg_ref[0] = (_rms(yg) * ng_ref[...]).astype(yg_ref.dtype)

    @pl.when(c == nc - 1)
    def _():
        for pr in range(npair):
            hfin_ref[0, pr] = ht_scr[pr].T


def ssm_branch(z, xbc, dt_pad, conv_past, h0, conv_w, conv_b, dtb_pad, alog_pad, dsk_exp, norm_g,
               *, lc, l_valid, n_groups, d_inner, d_state):
    bsz, length, _ = z.shape
    assert length % lc == 0 and lc % LANES == 0
    nc = length // lc
    gw = d_inner // n_groups
    assert gw % LANES == 0 and d_state == LANES
    npair = gw // LANES
    n_conv = conv_w.shape[0]
    xb0 = d_inner // d_state
    cb0 = xb0 + n_groups
    kern = functools.partial(_ssm_kernel, lc=lc, l_valid=l_valid, n_conv=n_conv)
    seq = lambda width, off: pl.BlockSpec((1, lc, width), lambda b, g, c: (b, c, off + g))
    par = lambda rows, width, off: pl.BlockSpec((rows, width), lambda b, g, c: (0, off + g))
    hist = lambda width, off: pl.BlockSpec((1, n_conv - 1, width), lambda b, g, c: (b, 0, off + g))
    state_spec = pl.BlockSpec((1, npair, LANES, d_state), lambda b, g, c: (b, g, 0, 0))
    yg, hfin = pl.pallas_call(
        kern,
        grid=(bsz, n_groups, nc),
        in_specs=[seq(gw, 0), seq(gw, 0), seq(d_state, xb0), seq(d_state, cb0), seq(LANES, 0),
                  pl.BlockSpec((1, 1, LANES), lambda b, g, c: (g, 0, 0)),
                  pl.BlockSpec((1, 1, LANES), lambda b, g, c: (g, 0, 0)),
                  par(1, gw, 0),
                  par(n_conv, gw, 0), par(n_conv, d_state, xb0), par(n_conv, d_state, cb0),
                  par(1, gw, 0), par(1, d_state, xb0), par(1, d_state, cb0),
                  hist(gw, 0), hist(d_state, xb0), hist(d_state, cb0),
                  state_spec, par(1, gw, 0)],
        out_specs=[seq(gw, 0), state_spec],
        out_shape=[jax.ShapeDtypeStruct((bsz, length, d_inner), BF16),
                   jax.ShapeDtypeStruct(h0.shape, F32)],
        scratch_shapes=[pltpu.VMEM((npair, d_state, LANES), F32),
                        pltpu.VMEM((lc + 2 * SUBLANES, gw), F32),
                        pltpu.VMEM((lc + 2 * SUBLANES, d_state), F32),
                        pltpu.VMEM((lc + 2 * SUBLANES, d_state), F32)],
        compiler_params=_cparams("parallel", "parallel", "arbitrary"),
        name="ssm_branch",
    )(z, xbc, xbc, xbc, dt_pad, dtb_pad, alog_pad, dsk_exp,
      conv_w, conv_w, conv_w, conv_b, conv_b, conv_b,
      conv_past, conv_past, conv_past, h0, norm_g)
    return yg, hfin


def _mix_kernel(x_ref, a_ref, y_ref, ga_ref, gs_ref, wa_ref, ws_ref, wo_ref, o_ref):
    o_a = jnp.dot(a_ref[...], wa_ref[...], preferred_element_type=F32)
    o_s = jnp.dot(y_ref[...], ws_ref[...], preferred_element_type=F32)
    mix = jax.nn.sigmoid(ga_ref[...]) * o_a + jax.nn.sigmoid(gs_ref[...]) * o_s
    o_ref[...] = x_ref[...] + jnp.dot(mix.astype(BF16), wo_ref[...], preferred_element_type=F32)


def mixer_out(x, attn, yg, gates, wa, ws, wo, tm=512):
    n, d = x.shape
    tm = min(tm, n)
    assert n % tm == 0 and gates.shape[1] == 2 * d
    row = lambda width, j=0: pl.BlockSpec((tm, width), lambda i, j=j: (i, j))
    full = lambda a: pl.BlockSpec(a.shape, lambda i: (0, 0))
    return pl.pallas_call(
        _mix_kernel,
        grid=(n // tm,),
        in_specs=[row(d), row(attn.shape[1]), row(yg.shape[1]), row(d, 0), row(d, 1),
                  full(wa), full(ws), full(wo)],
        out_specs=row(d),
        out_shape=jax.ShapeDtypeStruct((n, d), F32),
        compiler_params=_cparams("parallel"),
        name="mixer_out",
    )(x, attn, yg, gates, gates, wa, ws, wo)


def _staircase_pairs(k):
    return [(i, j) for i in range(k) for j in range(k) if (i + 1) * (j + 1) <= k]


def _topk_desc(x, k):
    vals = []
    for _ in range(k):
        m = jnp.max(x, axis=-1, keepdims=True)
        vals.append(m)
        x = jnp.where(x == m, -jnp.inf, x)
    return vals


def _router_kernel(x_ref, g_ref, wq_ref, k1_ref, k2_ref, ra_ref, rb_ref,
                   h_ref, s1t_ref, s2t_ref, e1t_ref, e2t_ref, tau_ref, *, n_heads, n_cand):
    t = x_ref.shape[0]
    h = (_rms(x_ref[...]) * g_ref[...]).astype(BF16)
    h_ref[...] = h
    q = jnp.dot(h, wq_ref[...], preferred_element_type=F32).astype(BF16)
    lane = lax.broadcasted_iota(jnp.int32, (t, LANES), 1)
    taus = jnp.zeros((t, LANES), F32)
    for hd in range(n_heads):
        qh = q[:, hd * LANES:(hd + 1) * LANES]
        s1 = lax.dot_general(qh, k1_ref[...], NT_DIMS, preferred_element_type=F32)
        s2 = lax.dot_general(qh, k2_ref[...], NT_DIMS, preferred_element_type=F32)
        v1 = _topk_desc(s1, P_TOPK)
        v2 = _topk_desc(s2, P_TOPK)
        v1m = jnp.zeros((t, LANES), F32)
        v2m = jnp.zeros((t, LANES), F32)
        for r in range(P_TOPK):
            v1m = jnp.where(lane == r, v1[r], v1m)
            v2m = jnp.where(lane == r, v2[r], v2m)
        cand = _dot3(v1m, ra_ref[...]) + _dot3(v2m, rb_ref[...])
        cand = jnp.where(lane < n_cand, cand, -jnp.inf)
        sc = _topk_desc(cand, P_TOPK)
        zsum = jnp.zeros_like(sc[0])
        for r in range(P_TOPK):
            zsum = zsum + jnp.exp(sc[r] - sc[0])
        taus = jnp.where(lane == hd, sc[P_TOPK - 1], taus)
        e1 = jnp.exp(s1 - v1[0])
        e2 = jnp.exp(s2 - v2[0]) / zsum
        s1t_ref[0, hd] = s1.T
        s2t_ref[0, hd] = s2.T
        e1t_ref[0, hd] = e1.T
        e2t_ref[0, hd] = e2.T
    tau_ref[0] = taus.T[:n_heads, :]


def peer_router(x, g, wq, k1p, k2p, ra, rb, *, n_heads, n_cand, tt):
    n, d = x.shape
    assert n % tt == 0
    nt = n // tt
    nk = k1p.shape[0]
    kern = functools.partial(_router_kernel, n_heads=n_heads, n_cand=n_cand)
    full = lambda a: pl.BlockSpec(a.shape, lambda i: (0,) * a.ndim)
    tspec = pl.BlockSpec((1, n_heads, nk, tt), lambda i: (i, 0, 0, 0))
    tshape = jax.ShapeDtypeStruct((nt, n_heads, nk, tt), F32)
    return pl.pallas_call(
        kern,
        grid=(nt,),
        in_specs=[pl.BlockSpec((tt, d), lambda i: (i, 0)), full(g), full(wq), full(k1p), full(k2p),
                  full(ra), full(rb)],
        out_specs=[pl.BlockSpec((tt, d), lambda i: (i, 0)), tspec, tspec, tspec, tspec,
                   pl.BlockSpec((1, n_heads, tt), lambda i: (i, 0, 0))],
        out_shape=[jax.ShapeDtypeStruct((n, d), BF16), tshape, tshape, tshape, tshape,
                   jax.ShapeDtypeStruct((nt, n_heads, tt), F32)],
        compiler_params=_cparams("parallel"),
        name="peer_router",
    )(x, g, wq, k1p, k2p, ra, rb)


def _gelu_tanh(x):
    c = math.sqrt(2.0 / math.pi)
    return x * (0.5 * (1.0 + jnp.tanh(c * (x + 0.044715 * (x * x * x)))))


def _experts_kernel(x_ref, h_ref, s1t_ref, s2t_ref, e1t_ref, e2t_ref, tau_ref, u_ref, vt_ref,
                    o_ref, acc_scr, a_scr, *, n_heads, rows_per_step):
    s = pl.program_id(1)
    ns = pl.num_programs(1)
    nk = s2t_ref.shape[2]
    tt = h_ref.shape[0]

    @pl.when(s == 0)
    def _():
        acc_scr[...] = jnp.zeros(acc_scr.shape, F32)

    pt = lax.dot_general(u_ref[...], h_ref[...], NT_DIMS, preferred_element_type=F32)
    for a in range(rows_per_step):
        a_glob = s * rows_per_step + a
        gsum = jnp.zeros((nk, tt), F32)
        for hd in range(n_heads):
            s1row = s1t_ref[0, hd, pl.ds(a_glob, 1), :]
            e1row = e1t_ref[0, hd, pl.ds(a_glob, 1), :]
            ssum = s1row + s2t_ref[0, hd]
            w = e1row * e2t_ref[0, hd]
            gsum = gsum + jnp.where(ssum >= tau_ref[0, hd:hd + 1, :], w, 0.0)
        p = pt[a * nk:(a + 1) * nk]
        a_scr[a * nk:(a + 1) * nk, :] = (_gelu_tanh(p) * gsum).astype(BF16)
    acc_scr[...] += jnp.dot(vt_ref[...], a_scr[...], preferred_element_type=F32)

    @pl.when(s == ns - 1)
    def _():
        o_ref[...] = x_ref[...] + acc_scr[...].T


def peer_experts(x, h, s1t, s2t, e1t, e2t, tau, u, vt, *, n_heads, rows_per_step):
    n, d = x.shape
    nt, _, nk, tt = s1t.shape
    n_exp = u.shape[0]
    eb = rows_per_step * nk
    assert n_exp % eb == 0 and n == nt * tt
    kern = functools.partial(_experts_kernel, n_heads=n_heads, rows_per_step=rows_per_step)
    tspec = pl.BlockSpec((1, n_heads, nk, tt), lambda i, s: (i, 0, 0, 0))
    return pl.pallas_call(
        kern,
        grid=(nt, n_exp // eb),
        in_specs=[pl.BlockSpec((tt, d), lambda i, s: (i, 0)),
                  pl.BlockSpec((tt, d), lambda i, s: (i, 0)),
                  tspec, tspec, tspec, tspec,
                  pl.BlockSpec((1, n_heads, tt), lambda i, s: (i, 0, 0)),
                  pl.BlockSpec((eb, d), lambda i, s: (s, 0)),
                  pl.BlockSpec((d, eb), lambda i, s: (0, s))],
        out_specs=pl.BlockSpec((tt, d), lambda i, s: (i, 0)),
        out_shape=jax.ShapeDtypeStruct((n, d), F32),
        scratch_shapes=[pltpu.VMEM((d, tt), F32), pltpu.VMEM((eb, tt), BF16)],
        compiler_params=_cparams("parallel", "arbitrary"),
        name="peer_experts",
    )(x, h, s1t, s2t, e1t, e2t, tau, u, vt)


def _ple_kernel(x_ref, p_ref, g_ref, wg_ref, wp_ref, gf_ref, o_ref, *, final):
    x = x_ref[...]
    hg = (_rms(x) * g_ref[...]).astype(BF16)
    gate = jax.nn.sigmoid(jnp.dot(hg, wg_ref[...], preferred_element_type=F32))
    pe = jnp.dot(p_ref[...].astype(BF16), wp_ref[...], preferred_element_type=F32)
    x = x + gate * pe
    if final:
        x = _rms(x) * gf_ref[...]
    o_ref[...] = x


def ple_out(x, p, g, wg, wp, gf, *, final, tm=512):
    n, d = x.shape
    tm = min(tm, n)
    assert n % tm == 0
    full = lambda a: pl.BlockSpec(a.shape, lambda i: (0, 0))
    return pl.pallas_call(
        functools.partial(_ple_kernel, final=final),
        grid=(n // tm,),
        in_specs=[pl.BlockSpec((tm, d), lambda i: (i, 0)), pl.BlockSpec((tm, p.shape[1]), lambda i: (i, 0)),
                  full(g), full(wg), full(wp), full(gf)],
        out_specs=pl.BlockSpec((tm, d), lambda i: (i, 0)),
        out_shape=jax.ShapeDtypeStruct((n, d), F32),
        compiler_params=_cparams("parallel"),
        name="ple_out",
    )(x, p, g, wg, wp, gf)


def _round_up(x, m):
    return -(-x // m) * m


def _layer(x, p_i, k_past, v_past, conv_past, ssm_past, q_off, lw, cfg, lam_init, final_g):
    bsz, length, d = x.shape
    n = bsz * length
    n_ah, a_dk, a_dv = cfg["a_heads"], cfg["a_dk"], cfg["a_dv"]
    d_inner, n_groups, d_state = cfg["d_inner"], cfg["s_groups"], cfg["d_state"]
    x2 = x.reshape(n, d)

    g_mix = lw["norm_mix_g"]
    (q,) = rms_matmul(x2, g_mix, lw["w_q"], [BF16])
    k_f32, k_bf = rms_matmul(x2, g_mix, lw["w_k"], [F32, BF16])
    v_f32, v_bf = rms_matmul(x2, g_mix, lw["w_v"], [F32, BF16])
    (z,) = rms_matmul(x2, g_mix, lw["w_z"], [F32])
    (xbc,) = rms_matmul(x2, g_mix, lw["w_xbc"], [F32])
    (dt_pad,) = rms_matmul(x2, g_mix, lw["w_dt"], [F32])
    (gates,) = rms_matmul(x2, g_mix, lw["w_gate"], [F32])

    hw = n_ah * LANES
    q3 = q.reshape(bsz, length, hw)
    k3 = k_bf.reshape(bsz, length, hw)
    v3 = v_bf.reshape(bsz, length, hw)
    if k_past is None:
        lk = length
        tq = tk = min(512, length)
        rq = min(128, tq)
    else:
        past = k_past.shape[1]
        lk = past + length
        lk_pad = _round_up(lk, LANES)
        zpad = jnp.zeros((bsz, lk_pad - lk, hw), BF16)
        k3 = jnp.concatenate([k_past.reshape(bsz, past, hw).astype(BF16), k3, zpad], axis=1)
        v3 = jnp.concatenate([v_past.reshape(bsz, past, hw).astype(BF16), v3, zpad], axis=1)
        tq, tk, rq = length, lk_pad, length
    attn = diff_attention(q3, k3, v3, lw["attn_par"], lw["attn_subln_g"], lq=length, lk=lk, q_off=q_off,
                          tq=tq, tk=tk, rq=rq, n_heads=n_ah, out_scale=1.0 - lam_init)

    lc = min(256, _round_up(length, LANES))
    l_pad = _round_up(length, lc)
    z3 = z.reshape(bsz, length, d_inner)
    xbc3 = xbc.reshape(bsz, length, -1)
    dt3 = dt_pad.reshape(bsz, length, -1)
    if l_pad != length:
        padt = lambda a: jnp.pad(a, ((0, 0), (0, l_pad - length), (0, 0)))
        z3p, xbc3p, dt3p = padt(z3), padt(xbc3), padt(dt3)
    else:
        z3p, xbc3p, dt3p = z3, xbc3, dt3
    n_conv = lw["conv_w"].shape[0]
    if conv_past is None:
        conv_past = jnp.zeros((bsz, n_conv - 1, xbc3.shape[-1]), F32)
    if ssm_past is None:
        ssm_past = jnp.zeros((bsz, cfg["s_heads"], cfg["s_head_p"], d_state), F32)
    h0 = ssm_past.astype(F32).reshape(bsz, -1, LANES, d_state)
    yg, hfin = ssm_branch(z3p, xbc3p, dt3p, conv_past.astype(F32), h0, lw["conv_w"], lw["conv_b"],
                          lw["dtb_pad"], lw["alog_pad"], lw["dsk_exp"], lw["ssm_norm_g"],
                          lc=lc, l_valid=length, n_groups=n_groups, d_inner=d_inner, d_state=d_state)
    yg = yg[:, :length].reshape(n, d_inner)
    ssm_new = hfin.reshape(ssm_past.shape)
    if length >= n_conv - 1:
        conv_new = xbc3[:, length - (n_conv - 1):]
    else:
        conv_new = jnp.concatenate([conv_past.astype(F32), xbc3], axis=1)[:, -(n_conv - 1):]

    x2 = mixer_out(x2, attn.reshape(n, hw), yg, gates, lw["w_attn_o"], lw["w_ssm_o"], lw["w_out"])

    tt = min(512, n)
    hp, s1t, s2t, e1t, e2t, tau = peer_router(
        x2, lw["norm_ffn_g"], lw["peer_wq"], lw["k1p"], lw["k2p"], lw["ra"], lw["rb"],
        n_heads=cfg["p_heads"], n_cand=cfg["n_cand"], tt=tt)
    x2 = peer_experts(x2, hp, s1t, s2t, e1t, e2t, tau, lw["peer_u"], lw["peer_vt"],
                      n_heads=cfg["p_heads"], rows_per_step=4)

    x2 = ple_out(x2, p_i.reshape(n, -1), lw["norm_ple_g"], lw["ple_w_gate"], lw["ple_w_proj"],
                 lw["norm_ple_g"] if final_g is None else final_g, final=final_g is not None)

    k_new = k_f32.reshape(bsz, length, n_ah, 2, a_dk)
    v_new = v_f32.reshape(bsz, length, n_ah, a_dv)
    return x2.reshape(bsz, length, d), k_new, v_new, conv_new, ssm_new


def kernel(x_prompt, x_sample, p_prompt, p_sample, cache_k, cache_v, state_conv, state_ssm, norm_mix_g, w_in, lam_q1, lam_k1, lam_q2, lam_k2, attn_subln_g, w_attn_o, conv_w, conv_b, dt_bias, a_log, d_skip, ssm_norm_g, w_ssm_o, w_out, norm_ffn_g, peer_wq, peer_k1, peer_k2, peer_u, peer_v, norm_ple_g, ple_w_gate, ple_w_proj, final_norm_g):
    depth = w_in.shape[0]
    d = x_prompt.shape[-1]
    a_heads, a_dk, a_dv = cache_k.shape[3], cache_k.shape[5], cache_v.shape[4]
    s_heads, s_head_p, d_state = state_ssm.shape[2:]
    d_inner = ssm_norm_g.shape[1]
    conv_dim = conv_w.shape[2]
    s_groups = (conv_dim - d_inner) // (2 * d_state)
    hpg = s_heads // s_groups
    p_nkeys, p_half = peer_k1.shape[1], peer_k1.shape[2]
    p_heads = peer_wq.shape[2] // (2 * p_half)
    assert 2 * a_dk == LANES and a_dv == LANES and 2 * s_head_p == LANES and d_state == LANES
    assert p_nkeys == LANES and 2 * p_half == LANES and hpg <= LANES
    pairs = _staircase_pairs(P_TOPK)
    assert len(pairs) <= LANES
    cfg = dict(a_heads=a_heads, a_dk=a_dk, a_dv=a_dv, d_inner=d_inner, s_groups=s_groups, d_state=d_state,
               s_heads=s_heads, s_head_p=s_head_p, p_heads=p_heads, n_cand=len(pairs))

    ra = np.zeros((LANES, LANES), np.float32)
    rb = np.zeros((LANES, LANES), np.float32)
    for pidx, (i, j) in enumerate(pairs):
        ra[i, pidx] = 1.0
        rb[j, pidx] = 1.0

    c_q = a_heads * 2 * a_dk
    c_v = a_heads * a_dv
    splits = np.cumsum([0, c_q, c_q, c_v, d_inner, conv_dim, s_heads, 2 * d])
    row = lambda a: a.astype(F32).reshape(1, -1)

    layers = []
    for i in range(depth):
        w = w_in[i]
        sl = lambda j: w[:, splits[j]:splits[j + 1]]
        w_dt = jnp.zeros((d, s_groups, LANES), F32).at[:, :, :hpg].set(sl(5).reshape(d, s_groups, hpg))
        grp = lambda a, fill: jnp.full((s_groups, 1, LANES), fill, F32).at[:, 0, :hpg].set(
            a.astype(F32).reshape(s_groups, hpg))
        f32 = jnp.float32
        lam_init = 0.8 - 0.6 * math.exp(-0.3 * i)
        lam = (jnp.exp(jnp.sum(lam_q1[i].astype(f32) * lam_k1[i].astype(f32)))
               - jnp.exp(jnp.sum(lam_q2[i].astype(f32) * lam_k2[i].astype(f32))) + lam_init)
        slopes = 2.0 ** (-8.0 * (jnp.arange(a_heads, dtype=f32) + 1.0) / a_heads)
        zeros_half = jnp.zeros((p_nkeys, p_half), F32)
        lw = dict(
            norm_mix_g=norm_mix_g[i],
            w_q=(sl(0) * (a_dk ** -0.5)).astype(BF16), w_k=sl(1).astype(BF16), w_v=sl(2).astype(BF16),
            w_z=sl(3).astype(BF16), w_xbc=sl(4).astype(BF16), w_gate=sl(6).astype(BF16),
            w_dt=w_dt.reshape(d, s_groups * LANES).astype(BF16),
            attn_par=jnp.concatenate([slopes, lam.reshape(1)]).astype(F32),
            attn_subln_g=attn_subln_g[i].astype(F32),
            w_attn_o=w_attn_o[i].astype(BF16), w_ssm_o=w_ssm_o[i].astype(BF16), w_out=w_out[i].astype(BF16),
            conv_w=conv_w[i].astype(F32), conv_b=row(conv_b[i]),
            dtb_pad=grp(dt_bias[i], 0.0), alog_pad=grp(a_log[i], 0.0),
            dsk_exp=row(jnp.repeat(d_skip[i], s_head_p)), ssm_norm_g=row(ssm_norm_g[i]),
            norm_ffn_g=row(norm_ffn_g[i]), peer_wq=peer_wq[i].astype(BF16),
            k1p=jnp.concatenate([peer_k1[i], zeros_half], axis=1).astype(BF16),
            k2p=jnp.concatenate([zeros_half, peer_k2[i]], axis=1).astype(BF16),
            ra=jnp.asarray(ra, BF16), rb=jnp.asarray(rb, BF16),
            peer_u=peer_u[i].astype(BF16), peer_vt=peer_v[i].astype(BF16).T,
            norm_ple_g=row(norm_ple_g[i]), ple_w_gate=ple_w_gate[i].astype(BF16),
            ple_w_proj=ple_w_proj[i].astype(BF16),
        )
        layers.append((lw, lam_init))

    def run_group(x, p, ck, cv, cconv, cssm, q_off):
        k_rows, v_rows, conv_rows, ssm_rows = [], [], [], []
        for i, (lw, lam_init) in enumerate(layers):
            last = i == depth - 1
            x, k_new, v_new, conv_new, ssm_new = _layer(
                x, p[i],
                None if ck is None else ck[i], None if cv is None else cv[i],
                None if cconv is None else cconv[i], None if cssm is None else cssm[i],
                q_off, lw, cfg, lam_init, row(final_norm_g) if last else None)
            k_rows.append(k_new)
            v_rows.append(v_new)
            conv_rows.append(conv_new)
            ssm_rows.append(ssm_new)
        return x, jnp.stack(k_rows), jnp.stack(v_rows), jnp.stack(conv_rows), jnp.stack(ssm_rows)

    y_p, k_p, v_p, conv_p, ssm_p = run_group(x_prompt, p_prompt, None, None, None, None, 0)
    y_s, k_s, v_s, conv_s, ssm_s = run_group(x_sample, p_sample, cache_k, cache_v, state_conv, state_ssm,
                                             cache_k.shape[2])
    return (y_p, y_s, k_p, v_p, conv_p, ssm_p, k_s, v_s, conv_s, ssm_s)
```

```python
import functools
import math

import numpy as np
import jax
import jax.numpy as jnp
from jax import lax
from jax.experimental import pallas as pl
from jax.experimental.pallas import tpu as pltpu

F32 = jnp.float32
BF16 = jnp.bfloat16

EPS = 1e-6
CHUNK = 64
NEG_INF = -1e30
P_TOPK = 16

LANES = 128
SUBLANES = 8
VMEM_LIMIT = 52 * 1024 * 1024

ATT_TQ = 512
ATT_TK = 512
ATT_RQ = 128
SSD_CHUNK = 256
PEER_TT = 512

NT_DIMS = (((1,), (1,)), ((), ()))
TN_DIMS = (((0,), (0,)), ((), ()))


def _cparams(*sem, flags=None):
    return pltpu.CompilerParams(dimension_semantics=sem, vmem_limit_bytes=VMEM_LIMIT, flags=flags)


def _rms(x):
    return x * lax.rsqrt(jnp.mean(x * x, axis=-1, keepdims=True) + EPS)


def _split3(x):
    hi = x.astype(BF16)
    r1 = x - hi.astype(F32)
    mid = r1.astype(BF16)
    lo = (r1 - mid.astype(F32)).astype(BF16)
    return hi, mid, lo


def _rms_matmul_kernel(x_ref, g_ref, w_ref, *o_refs):
    h = (_rms(x_ref[...]) * g_ref[...]).astype(BF16)
    y = jnp.dot(h, w_ref[...], preferred_element_type=F32)
    for o_ref in o_refs:
        o_ref[...] = y.astype(o_ref.dtype)


def rms_matmul(x, g, w, out_dtypes, tm=512):
    n, d = x.shape
    c = w.shape[1]
    tm = min(tm, n)
    tn = min(c, 1024)
    assert n % tm == 0 and c % tn == 0
    outs = pl.pallas_call(
        _rms_matmul_kernel,
        grid=(c // tn, n // tm),
        in_specs=[pl.BlockSpec((tm, d), lambda j, i: (i, 0)),
                  pl.BlockSpec((1, d), lambda j, i: (0, 0)),
                  pl.BlockSpec((d, tn), lambda j, i: (0, j))],
        out_specs=[pl.BlockSpec((tm, tn), lambda j, i: (i, j)) for _ in out_dtypes],
        out_shape=[jax.ShapeDtypeStruct((n, c), dt) for dt in out_dtypes],
        compiler_params=_cparams("parallel", "parallel"),
        name="rms_matmul",
    )(x, g.reshape(1, d), w)
    return outs


def _attn_kernel(qi_tab, ki_tab, par_ref, q_ref, k_ref, v_ref, g_ref, o_ref,
                 m_scr, l_scr, acc_scr, *, tq, tk, rq, q_off, lk, n_heads, out_scale):
    h = pl.program_id(1)
    step = pl.program_id(2)
    qi = qi_tab[step]
    ki = ki_tab[step]
    slope = par_ref[h]
    lam = par_ref[n_heads]
    dk = LANES // 2

    qs = q_off + qi * tq
    ks = ki * tk
    last_vis = (((qs + tq - 1) // CHUNK) * CHUNK + CHUNK - 1) // tk
    k_last = jnp.minimum((lk - 1) // tk, last_vis)

    @pl.when(ki == 0)
    def _():
        m_scr[...] = jnp.full(m_scr.shape, NEG_INF, F32)
        l_scr[...] = jnp.zeros(l_scr.shape, F32)
        acc_scr[...] = jnp.zeros(acc_scr.shape, F32)

    k = k_ref[0]
    v = v_ref[0]
    lane = lax.broadcasted_iota(jnp.int32, (rq, LANES), 1)

    def process(r, bias_fn):
        rows = slice(r * rq, (r + 1) * rq)
        q = q_ref[0, rows, :]
        zero = jnp.zeros_like(q)
        qq = jnp.concatenate([jnp.where(lane < dk, q, zero), jnp.where(lane >= dk, q, zero)], axis=0)
        s = lax.dot_general(qq, k, NT_DIMS, preferred_element_type=F32)
        bm = bias_fn(r)
        ps, alphas = [], []
        for j in range(2):
            sj = s[j * rq:(j + 1) * rq] + bm
            m_old = m_scr[j, rows, :]
            m_new = jnp.maximum(m_old, jnp.max(sj, axis=-1, keepdims=True))
            alpha = jnp.exp(m_old - m_new)
            p = jnp.exp(sj - jnp.tile(m_new, (1, tk // LANES)))
            l_scr[j, rows, :] = alpha * l_scr[j, rows, :] + jnp.sum(p, axis=-1, keepdims=True)
            m_scr[j, rows, :] = m_new
            alphas.append(alpha)
            ps.append(p.astype(BF16))
        pv = jnp.dot(jnp.concatenate(ps, axis=0), v, preferred_element_type=F32)
        for j in range(2):
            acc_scr[j, rows, :] = alphas[j] * acc_scr[j, rows, :] + pv[j * rq:(j + 1) * rq]

    def bias_row(r):
        kpos = ks + lax.broadcasted_iota(jnp.int32, (1, tk), 1)
        return slope * (kpos - qs).astype(F32)

    def bias_full(r):
        ii = r * rq + lax.broadcasted_iota(jnp.int32, (rq, tk), 0)
        qpos = qs + ii
        kpos = ks + lax.broadcasted_iota(jnp.int32, (rq, tk), 1)
        dist = jnp.abs(qpos - kpos)
        visible = (lax.shift_right_logical(kpos, 6) <= lax.shift_right_logical(qpos, 6)) & (kpos < lk)
        return jnp.where(visible, slope * (ii - dist).astype(F32), NEG_INF)

    n_sub = tq // rq
    simple = (ks + tk - 1 <= qs) & (ks + tk <= lk)

    @pl.when(simple)
    def _():
        for r in range(n_sub):
            process(r, bias_row)

    @pl.when(jnp.logical_not(simple) & (ki <= k_last))
    def _():
        for r in range(n_sub):
            process(r, bias_full)

    @pl.when(ki == k_last)
    def _():
        o = acc_scr[0] / l_scr[0] - lam * (acc_scr[1] / l_scr[1])
        o = _rms(o) * g_ref[...] * out_scale
        o_ref[0] = o.astype(o_ref.dtype)


def diff_attention(q, k, v, par, sub_g, *, lq, lk, q_off, tq, tk, rq, n_heads, out_scale):
    bsz = q.shape[0]
    lk_pad = k.shape[1]
    assert lq % tq == 0 and lk_pad % tk == 0 and tq % rq == 0 and CHUNK == 64
    nq, nk = lq // tq, lk_pad // tk
    pairs = []
    for qi in range(nq):
        qs = q_off + qi * tq
        k_last = min((lk - 1) // tk, (((qs + tq - 1) // CHUNK) * CHUNK + CHUNK - 1) // tk)
        pairs += [(qi, ki) for ki in range(k_last + 1)]
    qi_tab = jnp.asarray(np.array([p[0] for p in pairs], np.int32))
    ki_tab = jnp.asarray(np.array([p[1] for p in pairs], np.int32))
    kern = functools.partial(_attn_kernel, tq=tq, tk=tk, rq=rq, q_off=q_off, lk=lk,
                             n_heads=n_heads, out_scale=out_scale)
    grid_spec = pltpu.PrefetchScalarGridSpec(
        num_scalar_prefetch=2,
        grid=(bsz, n_heads, len(pairs)),
        in_specs=[pl.BlockSpec(memory_space=pltpu.SMEM),
                  pl.BlockSpec((1, tq, LANES), lambda b, h, s, qt, kt: (b, qt[s], h)),
                  pl.BlockSpec((1, tk, LANES), lambda b, h, s, qt, kt: (b, kt[s], h)),
                  pl.BlockSpec((1, tk, LANES), lambda b, h, s, qt, kt: (b, kt[s], h)),
                  pl.BlockSpec((1, LANES), lambda b, h, s, qt, kt: (0, 0))],
        out_specs=pl.BlockSpec((1, tq, LANES), lambda b, h, s, qt, kt: (b, qt[s], h)),
        scratch_shapes=[pltpu.VMEM((2, tq, LANES), F32), pltpu.VMEM((2, tq, LANES), F32),
                        pltpu.VMEM((2, tq, LANES), F32)])
    return pl.pallas_call(
        kern, grid_spec=grid_spec,
        out_shape=jax.ShapeDtypeStruct((bsz, lq, n_heads * LANES), BF16),
        compiler_params=_cparams("parallel", "parallel", "arbitrary"),
        name="diff_attention",
    )(qi_tab, ki_tab, par, q, k, v, sub_g.reshape(1, LANES))


def _ssm_kernel(z_ref, x_ref, b_ref, c_ref, dt_ref, dtb_ref, alog_ref, dsk_ref,
                cwx_ref, cwb_ref, cwc_ref, cbx_ref, cbb_ref, cbc_ref,
                px_ref, pb_ref, pc_ref, h0_ref, ng_ref,
                yg_ref, hfin_ref, ht_scr, xpx, xpb, xpc, *, lc, l_valid, n_conv):
    c = pl.program_id(2)
    nc = pl.num_programs(2)
    npair = ht_scr.shape[0]
    pad = SUBLANES
    nh = n_conv - 1

    @pl.when(c == 0)
    def _():
        for pr in range(npair):
            ht_scr[pr] = h0_ref[0, pr].T
        xpx[pad - nh:pad, :] = px_ref[0]
        xpb[pad - nh:pad, :] = pb_ref[0]
        xpc[pad - nh:pad, :] = pc_ref[0]

    @pl.when(c > 0)
    def _():
        xpx[pad - nh:pad, :] = xpx[pad + lc - nh:pad + lc, :]
        xpb[pad - nh:pad, :] = xpb[pad + lc - nh:pad + lc, :]
        xpc[pad - nh:pad, :] = xpc[pad + lc - nh:pad + lc, :]

    xpx[pad:pad + lc, :] = x_ref[0]
    xpb[pad:pad + lc, :] = b_ref[0]
    xpc[pad:pad + lc, :] = c_ref[0]

    def conv_silu(xp, cw_ref, cb_ref):
        acc = cb_ref[...]
        for w in range(n_conv):
            acc = acc + cw_ref[w:w + 1, :] * xp[pad - nh + w:pad - nh + w + lc, :]
        return acc * jax.nn.sigmoid(acc)

    xs = conv_silu(xpx, cwx_ref, cbx_ref)
    bm = conv_silu(xpb, cwb_ref, cbb_ref).astype(BF16)
    cm = conv_silu(xpc, cwc_ref, cbc_ref).astype(BF16)

    t_col = c * lc + lax.broadcasted_iota(jnp.int32, (lc, LANES), 0)
    dt_c = jax.nn.softplus(dt_ref[0] + dtb_ref[0])
    dt_c = jnp.where(t_col < l_valid, dt_c, 0.0)
    da_c = dt_c * (-jnp.exp(alog_ref[0]))
    ri = lax.broadcasted_iota(jnp.int32, (lc, lc), 0)
    ci = lax.broadcasted_iota(jnp.int32, (lc, lc), 1)
    causal = ri >= ci
    tril = jnp.where(causal, 1.0, 0.0).astype(BF16)
    acum_c = None
    for piece in _split3(da_c):
        t = jnp.dot(tril, piece, preferred_element_type=F32)
        acum_c = t if acum_c is None else acum_c + t
    acum_r = acum_c.T
    dt_r = dt_c.T
    a_last = acum_c[lc - 1:lc, :]
    wdec_c = jnp.exp(a_last - acum_c) * dt_c
    eac_c = jnp.exp(acum_c)
    ea_last = jnp.exp(a_last)

    cb = lax.dot_general(cm, bm, NT_DIMS, preferred_element_type=F32)
    lane = lax.broadcasted_iota(jnp.int32, (lc, LANES), 1)
    lane1 = lax.broadcasted_iota(jnp.int32, (1, LANES), 1)
    half = LANES // 2
    ys = []
    for pr in range(npair):
        xp_f = xs[:, pr * LANES:(pr + 1) * LANES]
        y = dsk_ref[:, pr * LANES:(pr + 1) * LANES] * xp_f
        for e in range(2):
            hh = 2 * pr + e
            seg = acum_c[:, hh:hh + 1] - acum_r[hh:hh + 1, :]
            lmat = jnp.exp(jnp.where(causal, seg, NEG_INF))
            m = (cb * lmat * dt_r[hh:hh + 1, :]).astype(BF16)
            sel = (lane < half) if e == 0 else (lane >= half)
            xh = jnp.where(sel, xp_f, 0.0).astype(BF16)
            y = y + jnp.dot(m, xh, preferred_element_type=F32)
        h_a, h_b = 2 * pr, 2 * pr + 1
        scale = jnp.where(lane < half, eac_c[:, h_a:h_a + 1], eac_c[:, h_b:h_b + 1])
        ht = ht_scr[pr]
        y = y + jnp.dot(cm, ht.astype(BF16), preferred_element_type=F32) * scale
        wd = jnp.where(lane < half, wdec_c[:, h_a:h_a + 1], wdec_c[:, h_b:h_b + 1])
        dec = jnp.where(lane1 < half, ea_last[:, h_a:h_a + 1], ea_last[:, h_b:h_b + 1])
        upd = lax.dot_general(bm, (wd * xp_f).astype(BF16), TN_DIMS, preferred_element_type=F32)
        ht_scr[pr] = dec * ht + upd
        ys.append(y)
    y = jnp.concatenate(ys, axis=-1)

    zz = z_ref[0]
    yg = y * (zz * jax.nn.sigmoid(zz))
    yg_ref[0] = (_rms(yg) * ng_ref[...]).astype(yg_ref.dtype)

    @pl.when(c == nc - 1)
    def _():
        for pr in range(npair):
            hfin_ref[0, pr] = ht_scr[pr].T


def ssm_branch(z, xbc, dt_pad, conv_past, h0, conv_w, conv_b, dtb_pad, alog_pad, dsk_exp, norm_g,
               *, lc, l_valid, n_groups, d_inner, d_state):
    bsz, length, _ = z.shape
    assert length % lc == 0 and lc % LANES == 0
    nc = length // lc
    gw = d_inner // n_groups
    assert gw % LANES == 0 and d_state == LANES
    npair = gw // LANES
    n_conv = conv_w.shape[0]
    xb0 = d_inner // d_state
    cb0 = xb0 + n_groups
    kern = functools.partial(_ssm_kernel, lc=lc, l_valid=l_valid, n_conv=n_conv)
    seq = lambda width, off: pl.BlockSpec((1, lc, width), lambda b, g, c: (b, c, off + g))
    par = lambda rows, width, off: pl.BlockSpec((rows, width), lambda b, g, c: (0, off + g))
    hist = lambda width, off: pl.BlockSpec((1, n_conv - 1, width), lambda b, g, c: (b, 0, off + g))
    state_spec = pl.BlockSpec((1, npair, LANES, d_state), lambda b, g, c: (b, g, 0, 0))
    yg, hfin = pl.pallas_call(
        kern,
        grid=(bsz, n_groups, nc),
        in_specs=[seq(gw, 0), seq(gw, 0), seq(d_state, xb0), seq(d_state, cb0), seq(LANES, 0),
                  pl.BlockSpec((1, 1, LANES), lambda b, g, c: (g, 0, 0)),
                  pl.BlockSpec((1, 1, LANES), lambda b, g, c: (g, 0, 0)),
                  par(1, gw, 0),
                  par(n_conv, gw, 0), par(n_conv, d_state, xb0), par(n_conv, d_state, cb0),
                  par(1, gw, 0), par(1, d_state, xb0), par(1, d_state, cb0),
                  hist(gw, 0), hist(d_state, xb0), hist(d_state, cb0),
                  state_spec, par(1, gw, 0)],
        out_specs=[seq(gw, 0), state_spec],
        out_shape=[jax.ShapeDtypeStruct((bsz, length, d_inner), BF16),
                   jax.ShapeDtypeStruct(h0.shape, F32)],
        scratch_shapes=[pltpu.VMEM((npair, d_state, LANES), F32),
                        pltpu.VMEM((lc + 2 * SUBLANES, gw), F32),
                        pltpu.VMEM((lc + 2 * SUBLANES, d_state), F32),
                        pltpu.VMEM((lc + 2 * SUBLANES, d_state), F32)],
        compiler_params=_cparams("parallel", "parallel", "arbitrary"),
        name="ssm_branch",
    )(z, xbc, xbc, xbc, dt_pad, dtb_pad, alog_pad, dsk_exp,
      conv_w, conv_w, conv_w, conv_b, conv_b, conv_b,
      conv_past, conv_past, conv_past, h0, norm_g)
    return yg, hfin


def _mix_kernel(x_ref, a_ref, y_ref, ga_ref, gs_ref, wa_ref, ws_ref, wo_ref, o_ref):
    o_a = jnp.dot(a_ref[...], wa_ref[...], preferred_element_type=F32)
    o_s = jnp.dot(y_ref[...], ws_ref[...], preferred_element_type=F32)
    mix = jax.nn.sigmoid(ga_ref[...]) * o_a + jax.nn.sigmoid(gs_ref[...]) * o_s
    o_ref[...] = x_ref[...] + jnp.dot(mix.astype(BF16), wo_ref[...], preferred_element_type=F32)


def mixer_out(x, attn, yg, gates, wa, ws, wo, tm=512):
    n, d = x.shape
    tm = min(tm, n)
    assert n % tm == 0 and gates.shape[1] == 2 * d
    row = lambda width, j=0: pl.BlockSpec((tm, width), lambda i, j=j: (i, j))
    full = lambda a: pl.BlockSpec(a.shape, lambda i: (0, 0))
    return pl.pallas_call(
        _mix_kernel,
        grid=(n // tm,),
        in_specs=[row(d), row(attn.shape[1]), row(yg.shape[1]), row(d, 0), row(d, 1),
                  full(wa), full(ws), full(wo)],
        out_specs=row(d),
        out_shape=jax.ShapeDtypeStruct((n, d), F32),
        compiler_params=_cparams("parallel"),
        name="mixer_out",
    )(x, attn, yg, gates, gates, wa, ws, wo)


def _staircase_pairs(k):
    return [(i, j) for i in range(k) for j in range(k) if (i + 1) * (j + 1) <= k]


def _oddeven_merge(lo, hi, r):
    step = r * 2
    if step < hi - lo:
        yield from _oddeven_merge(lo, hi, step)
        yield from _oddeven_merge(lo + r, hi, step)
        yield from [(i, i + r) for i in range(lo + r, hi - r, step)]
    else:
        yield (lo, lo + r)


def _oddeven_sort_pairs(lo, hi):
    if hi - lo >= 1:
        mid = lo + (hi - lo) // 2
        yield from _oddeven_sort_pairs(lo, mid)
        yield from _oddeven_sort_pairs(mid + 1, hi)
        yield from _oddeven_merge(lo, hi, 1)


def _sort_desc(vals):
    vals = list(vals)
    for i, j in _oddeven_sort_pairs(0, len(vals) - 1):
        vals[i], vals[j] = jnp.maximum(vals[i], vals[j]), jnp.minimum(vals[i], vals[j])
    return vals


def _bitonic_merge_desc(vals):
    vals = list(vals)
    n = len(vals)
    d = n // 2
    while d >= 1:
        for i in range(n):
            if i & d == 0:
                vals[i], vals[i + d] = jnp.maximum(vals[i], vals[i + d]), jnp.minimum(vals[i], vals[i + d])
        d //= 2
    return vals


def _merge_sublanes_top(vals):
    n = len(vals)
    shift = SUBLANES // 2
    while shift >= 1:
        other = [pltpu.roll(v, shift, 0) for v in vals]
        vals = _bitonic_merge_desc([jnp.maximum(vals[i], other[n - 1 - i]) for i in range(n)])
        shift //= 2
    return vals


def _router_kernel(x_ref, g_ref, wqt_ref, k1_ref, k2_ref,
                   h_ref, thr_ref, s2t_ref, e1t_ref, e2t_ref, qt_scr, *, n_heads, pairs):
    tt = x_ref.shape[0]
    nk = k1_ref.shape[0]
    h = (_rms(x_ref[...]) * g_ref[...]).astype(BF16)
    h_ref[...] = h
    qt_scr[...] = lax.dot_general(wqt_ref[...], h, NT_DIMS, preferred_element_type=F32).astype(BF16)
    sub = lax.broadcasted_iota(jnp.int32, (SUBLANES, LANES), 0)
    n_cv = -(-len(pairs) // SUBLANES)
    n_cv_pad = pl.next_power_of_2(n_cv)

    def head(hd, carry):
        qh = qt_scr[pl.ds(pl.multiple_of(hd * nk, nk), nk), :]
        s1 = jnp.dot(k1_ref[...], qh, preferred_element_type=F32)
        s2 = jnp.dot(k2_ref[...], qh, preferred_element_type=F32)
        s2t_ref[0, hd] = s2
        for lg in range(tt // LANES):
            ls = slice(lg * LANES, (lg + 1) * LANES)
            s1cols = [s1[r * SUBLANES:(r + 1) * SUBLANES, ls] for r in range(nk // SUBLANES)]
            s2cols = [s2[r * SUBLANES:(r + 1) * SUBLANES, ls] for r in range(nk // SUBLANES)]
            v1 = _merge_sublanes_top(_sort_desc(s1cols))[:P_TOPK]
            v2 = _merge_sublanes_top(_sort_desc(s2cols))[:P_TOPK]
            sums = {(i, j): v1[i] + v2[j] for (i, j) in pairs}
            cvs = []
            for c in range(n_cv_pad):
                cv = jnp.full((SUBLANES, LANES), -jnp.inf, F32)
                for k in range(SUBLANES):
                    pidx = c * SUBLANES + k
                    if pidx < len(pairs):
                        cv = jnp.where(sub == k, sums[pairs[pidx]], cv)
                cvs.append(cv)
            top = _merge_candidates(_sort_desc(cvs))
            tau = top[P_TOPK - 1]
            zsum = jnp.zeros((SUBLANES, LANES), F32)
            for r in range(P_TOPK):
                zsum = zsum + jnp.exp(top[r] - top[0])
            thetas = []
            for i in range(P_TOPK):
                theta = jnp.full((SUBLANES, LANES), jnp.inf, F32)
                for (pi, pj) in pairs:
                    if pi == i:
                        theta = jnp.where(sums[(pi, pj)] >= tau, v2[pj], theta)
                thetas.append(theta)
            for r in range(nk // SUBLANES):
                thr = jnp.full((SUBLANES, LANES), jnp.inf, F32)
                for i in range(P_TOPK):
                    thr = jnp.where(s1cols[r] == v1[i], thetas[i], thr)
                thr_ref[0, hd, r * SUBLANES:(r + 1) * SUBLANES, ls] = thr
            e1t_ref[0, hd, :, ls] = jnp.exp(s1[:, ls] - v1[0][:1])
            e2t_ref[0, hd, :, ls] = jnp.exp(s2[:, ls] - v2[0][:1]) / zsum[:1]
        return carry

    lax.fori_loop(0, n_heads, head, 0)


def _merge_candidates(cvs):
    assert len(cvs) == SUBLANES and P_TOPK == 2 * SUBLANES
    other = [pltpu.roll(v, SUBLANES // 2, 0) for v in cvs]
    vals = _bitonic_merge_desc(cvs + other[::-1])
    for shift in (SUBLANES // 4, SUBLANES // 8):
        oth = [pltpu.roll(v, shift, 0) for v in vals]
        vals = _bitonic_merge_desc([jnp.maximum(vals[i], oth[P_TOPK - 1 - i]) for i in range(P_TOPK)])
    return vals


def peer_router(x, g, wqt, k1p, k2p, *, n_heads, pairs, tt):
    n, d = x.shape
    assert n % tt == 0 and tt % LANES == 0
    nt = n // tt
    nk = k1p.shape[0]
    kern = functools.partial(_router_kernel, n_heads=n_heads, pairs=pairs)
    full = lambda a: pl.BlockSpec(a.shape, lambda i: (0,) * a.ndim)
    tspec = pl.BlockSpec((1, n_heads, nk, tt), lambda i: (i, 0, 0, 0))
    tshape = jax.ShapeDtypeStruct((nt, n_heads, nk, tt), F32)
    return pl.pallas_call(
        kern,
        grid=(nt,),
        in_specs=[pl.BlockSpec((tt, d), lambda i: (i, 0)), full(g), full(wqt), full(k1p), full(k2p)],
        out_specs=[pl.BlockSpec((tt, d), lambda i: (i, 0)), tspec, tspec, tspec, tspec],
        out_shape=[jax.ShapeDtypeStruct((n, d), BF16), tshape, tshape, tshape, tshape],
        scratch_shapes=[pltpu.VMEM((wqt.shape[0], tt), BF16)],
        compiler_params=_cparams("parallel"),
        name="peer_router",
    )(x, g, wqt, k1p, k2p)


def _gelu_tanh(x):
    c = math.sqrt(2.0 / math.pi)
    return x * (0.5 * (1.0 + jnp.tanh(c * (x + 0.044715 * (x * x * x)))))


def _experts_kernel(x_ref, h_ref, thr_ref, s2t_ref, e1r_ref, e2t_ref, u_ref, vt_ref,
                    o_ref, acc_scr, a_scr, *, n_heads, rows_per_step, bh):
    s = pl.program_id(1)
    nk = s2t_ref.shape[2]
    tt = h_ref.shape[0]

    @pl.when(s == 0)
    def _():
        acc_scr[...] = jnp.zeros(acc_scr.shape, F32)

    pt = lax.dot_general(u_ref[...], h_ref[...], NT_DIMS, preferred_element_type=F32)
    for a in range(rows_per_step):
        for lg in range(tt // LANES):
            ls = slice(lg * LANES, (lg + 1) * LANES)
            for b0 in range(0, nk, bh):
                gsum = jnp.zeros((bh, LANES), F32)
                for hd in range(n_heads):
                    keep = s2t_ref[0, hd, b0:b0 + bh, ls] >= thr_ref[0, hd, a:a + 1, ls]
                    w = e1r_ref[0, hd, a:a + 1, ls] * e2t_ref[0, hd, b0:b0 + bh, ls]
                    gsum = gsum + jnp.where(keep, w, 0.0)
                r0 = a * nk + b0
                a_scr[r0:r0 + bh, ls] = (_gelu_tanh(pt[r0:r0 + bh, ls]) * gsum).astype(BF16)
    acc_scr[...] += jnp.dot(vt_ref[...], a_scr[...], preferred_element_type=F32)

    @pl.when(s == pl.num_programs(1) - 1)
    def _():
        o_ref[...] = x_ref[...] + acc_scr[...].T


def peer_experts(x, h, thr, s2t, e1t, e2t, u, vt, *, n_heads):
    n, d = x.shape
    nt, _, nk, tt = s2t.shape
    n_exp = u.shape[0]
    rows_per_step = SUBLANES
    bh = 8 * SUBLANES
    eb = rows_per_step * nk
    assert n_exp % eb == 0 and n == nt * tt and nk % bh == 0
    kern = functools.partial(_experts_kernel, n_heads=n_heads, rows_per_step=rows_per_step, bh=bh)
    tspec = pl.BlockSpec((1, n_heads, nk, tt), lambda i, s: (i, 0, 0, 0))
    rspec = pl.BlockSpec((1, n_heads, rows_per_step, tt), lambda i, s: (i, 0, s, 0))
    return pl.pallas_call(
        kern,
        grid=(nt, n_exp // eb),
        in_specs=[pl.BlockSpec((tt, d), lambda i, s: (i, 0)),
                  pl.BlockSpec((tt, d), lambda i, s: (i, 0)),
                  rspec, tspec, rspec, tspec,
                  pl.BlockSpec((eb, d), lambda i, s: (s, 0)),
                  pl.BlockSpec((d, eb), lambda i, s: (0, s))],
        out_specs=pl.BlockSpec((tt, d), lambda i, s: (i, 0)),
        out_shape=jax.ShapeDtypeStruct((n, d), F32),
        scratch_shapes=[pltpu.VMEM((d, tt), F32), pltpu.VMEM((eb, tt), BF16)],
        compiler_params=_cparams("parallel", "arbitrary"),
        name="peer_experts",
    )(x, h, thr, s2t, e1t, e2t, u, vt)


def _ple_kernel(x_ref, p_ref, g_ref, wg_ref, wp_ref, gf_ref, o_ref, *, final):
    x = x_ref[...]
    hg = (_rms(x) * g_ref[...]).astype(BF16)
    gate = jax.nn.sigmoid(jnp.dot(hg, wg_ref[...], preferred_element_type=F32))
    pe = jnp.dot(p_ref[...].astype(BF16), wp_ref[...], preferred_element_type=F32)
    x = x + gate * pe
    if final:
        x = _rms(x) * gf_ref[...]
    o_ref[...] = x


def ple_out(x, p, g, wg, wp, gf, *, final, tm=512):
    n, d = x.shape
    tm = min(tm, n)
    assert n % tm == 0
    full = lambda a: pl.BlockSpec(a.shape, lambda i: (0, 0))
    return pl.pallas_call(
        functools.partial(_ple_kernel, final=final),
        grid=(n // tm,),
        in_specs=[pl.BlockSpec((tm, d), lambda i: (i, 0)), pl.BlockSpec((tm, p.shape[1]), lambda i: (i, 0)),
                  full(g), full(wg), full(wp), full(gf)],
        out_specs=pl.BlockSpec((tm, d), lambda i: (i, 0)),
        out_shape=jax.ShapeDtypeStruct((n, d), F32),
        compiler_params=_cparams("parallel"),
        name="ple_out",
    )(x, p, g, wg, wp, gf)


def _round_up(x, m):
    return -(-x // m) * m


def _layer(x, p_i, k_past, v_past, conv_past, ssm_past, q_off, lw, cfg, lam_init, final_g):
    bsz, length, d = x.shape
    n = bsz * length
    n_ah, a_dk, a_dv = cfg["a_heads"], cfg["a_dk"], cfg["a_dv"]
    d_inner, n_groups, d_state = cfg["d_inner"], cfg["s_groups"], cfg["d_state"]
    x2 = x.reshape(n, d)

    g_mix = lw["norm_mix_g"]
    (q,) = rms_matmul(x2, g_mix, lw["w_q"], [BF16])
    k_f32, k_bf = rms_matmul(x2, g_mix, lw["w_k"], [F32, BF16])
    v_f32, v_bf = rms_matmul(x2, g_mix, lw["w_v"], [F32, BF16])
    (z,) = rms_matmul(x2, g_mix, lw["w_z"], [F32])
    (xbc,) = rms_matmul(x2, g_mix, lw["w_xbc"], [F32])
    (dt_pad,) = rms_matmul(x2, g_mix, lw["w_dt"], [F32])
    (gates,) = rms_matmul(x2, g_mix, lw["w_gate"], [F32])

    hw = n_ah * LANES
    q3 = q.reshape(bsz, length, hw)
    k3 = k_bf.reshape(bsz, length, hw)
    v3 = v_bf.reshape(bsz, length, hw)
    if k_past is None:
        lk = length
        tq, tk = min(ATT_TQ, length), min(ATT_TK, length)
        rq = min(ATT_RQ, tq)
    else:
        past = k_past.shape[1]
        lk = past + length
        lk_pad = _round_up(lk, LANES)
        zpad = jnp.zeros((bsz, lk_pad - lk, hw), BF16)
        k3 = jnp.concatenate([k_past.reshape(bsz, past, hw).astype(BF16), k3, zpad], axis=1)
        v3 = jnp.concatenate([v_past.reshape(bsz, past, hw).astype(BF16), v3, zpad], axis=1)
        tq, tk, rq = length, lk_pad, length
    attn = diff_attention(q3, k3, v3, lw["attn_par"], lw["attn_subln_g"], lq=length, lk=lk, q_off=q_off,
                          tq=tq, tk=tk, rq=rq, n_heads=n_ah, out_scale=1.0 - lam_init)

    lc = min(SSD_CHUNK, _round_up(length, LANES))
    l_pad = _round_up(length, lc)
    z3 = z.reshape(bsz, length, d_inner)
    xbc3 = xbc.reshape(bsz, length, -1)
    dt3 = dt_pad.reshape(bsz, length, -1)
    if l_pad != length:
        padt = lambda a: jnp.pad(a, ((0, 0), (0, l_pad - length), (0, 0)))
        z3p, xbc3p, dt3p = padt(z3), padt(xbc3), padt(dt3)
    else:
        z3p, xbc3p, dt3p = z3, xbc3, dt3
    n_conv = lw["conv_w"].shape[0]
    if conv_past is None:
        conv_past = jnp.zeros((bsz, n_conv - 1, xbc3.shape[-1]), F32)
    if ssm_past is None:
        ssm_past = jnp.zeros((bsz, cfg["s_heads"], cfg["s_head_p"], d_state), F32)
    h0 = ssm_past.astype(F32).reshape(bsz, -1, LANES, d_state)
    yg, hfin = ssm_branch(z3p, xbc3p, dt3p, conv_past.astype(F32), h0, lw["conv_w"], lw["conv_b"],
                          lw["dtb_pad"], lw["alog_pad"], lw["dsk_exp"], lw["ssm_norm_g"],
                          lc=lc, l_valid=length, n_groups=n_groups, d_inner=d_inner, d_state=d_state)
    yg = yg[:, :length].reshape(n, d_inner)
    ssm_new = hfin.reshape(ssm_past.shape)
    if length >= n_conv - 1:
        conv_new = xbc3[:, length - (n_conv - 1):]
    else:
        conv_new = jnp.concatenate([conv_past.astype(F32), xbc3], axis=1)[:, -(n_conv - 1):]

    x2 = mixer_out(x2, attn.reshape(n, hw), yg, gates, lw["w_attn_o"], lw["w_ssm_o"], lw["w_out"])

    tt = min(PEER_TT, n)
    hp, thr, s2t, e1t, e2t = peer_router(
        x2, lw["norm_ffn_g"], lw["peer_wqt"], lw["k1p"], lw["k2p"],
        n_heads=cfg["p_heads"], pairs=cfg["pairs"], tt=tt)
    x2 = peer_experts(x2, hp, thr, s2t, e1t, e2t, lw["peer_u"], lw["peer_vt"], n_heads=cfg["p_heads"])

    x2 = ple_out(x2, p_i.reshape(n, -1), lw["norm_ple_g"], lw["ple_w_gate"], lw["ple_w_proj"],
                 lw["norm_ple_g"] if final_g is None else final_g, final=final_g is not None)

    k_new = k_f32.reshape(bsz, length, n_ah, 2, a_dk)
    v_new = v_f32.reshape(bsz, length, n_ah, a_dv)
    return x2.reshape(bsz, length, d), k_new, v_new, conv_new, ssm_new


def kernel(x_prompt, x_sample, p_prompt, p_sample, cache_k, cache_v, state_conv, state_ssm, norm_mix_g, w_in, lam_q1, lam_k1, lam_q2, lam_k2, attn_subln_g, w_attn_o, conv_w, conv_b, dt_bias, a_log, d_skip, ssm_norm_g, w_ssm_o, w_out, norm_ffn_g, peer_wq, peer_k1, peer_k2, peer_u, peer_v, norm_ple_g, ple_w_gate, ple_w_proj, final_norm_g):
    depth = w_in.shape[0]
    d = x_prompt.shape[-1]
    a_heads, a_dk, a_dv = cache_k.shape[3], cache_k.shape[5], cache_v.shape[4]
    s_heads, s_head_p, d_state = state_ssm.shape[2:]
    d_inner = ssm_norm_g.shape[1]
    conv_dim = conv_w.shape[2]
    s_groups = (conv_dim - d_inner) // (2 * d_state)
    hpg = s_heads // s_groups
    p_nkeys, p_half = peer_k1.shape[1], peer_k1.shape[2]
    p_heads = peer_wq.shape[2] // (2 * p_half)
    assert 2 * a_dk == LANES and a_dv == LANES and 2 * s_head_p == LANES and d_state == LANES
    assert p_nkeys == LANES and 2 * p_half == LANES and hpg <= LANES
    pairs = _staircase_pairs(P_TOPK)
    assert len(pairs) <= SUBLANES * SUBLANES
    cfg = dict(a_heads=a_heads, a_dk=a_dk, a_dv=a_dv, d_inner=d_inner, s_groups=s_groups, d_state=d_state,
               s_heads=s_heads, s_head_p=s_head_p, p_heads=p_heads, pairs=pairs)

    c_q = a_heads * 2 * a_dk
    c_v = a_heads * a_dv
    splits = np.cumsum([0, c_q, c_q, c_v, d_inner, conv_dim, s_heads, 2 * d])
    row = lambda a: a.astype(F32).reshape(1, -1)

    layers = []
    for i in range(depth):
        w = w_in[i]
        sl = lambda j: w[:, splits[j]:splits[j + 1]]
        w_dt = jnp.zeros((d, s_groups, LANES), F32).at[:, :, :hpg].set(sl(5).reshape(d, s_groups, hpg))
        grp = lambda a, fill: jnp.full((s_groups, 1, LANES), fill, F32).at[:, 0, :hpg].set(
            a.astype(F32).reshape(s_groups, hpg))
        f32 = jnp.float32
        lam_init = 0.8 - 0.6 * math.exp(-0.3 * i)
        lam = (jnp.exp(jnp.sum(lam_q1[i].astype(f32) * lam_k1[i].astype(f32)))
               - jnp.exp(jnp.sum(lam_q2[i].astype(f32) * lam_k2[i].astype(f32))) + lam_init)
        slopes = 2.0 ** (-8.0 * (jnp.arange(a_heads, dtype=f32) + 1.0) / a_heads)
        zeros_half = jnp.zeros((p_nkeys, p_half), F32)
        lw = dict(
            norm_mix_g=norm_mix_g[i],
            w_q=(sl(0) * (a_dk ** -0.5)).astype(BF16), w_k=sl(1).astype(BF16), w_v=sl(2).astype(BF16),
            w_z=sl(3).astype(BF16), w_xbc=sl(4).astype(BF16), w_gate=sl(6).astype(BF16),
            w_dt=w_dt.reshape(d, s_groups * LANES).astype(BF16),
            attn_par=jnp.concatenate([slopes, lam.reshape(1)]).astype(F32),
            attn_subln_g=attn_subln_g[i].astype(F32),
            w_attn_o=w_attn_o[i].astype(BF16), w_ssm_o=w_ssm_o[i].astype(BF16), w_out=w_out[i].astype(BF16),
            conv_w=conv_w[i].astype(F32), conv_b=row(conv_b[i]),
            dtb_pad=grp(dt_bias[i], 0.0), alog_pad=grp(a_log[i], 0.0),
            dsk_exp=row(jnp.repeat(d_skip[i], s_head_p)), ssm_norm_g=row(ssm_norm_g[i]),
            norm_ffn_g=row(norm_ffn_g[i]), peer_wqt=peer_wq[i].astype(BF16).T,
            k1p=jnp.concatenate([peer_k1[i], zeros_half], axis=1).astype(BF16),
            k2p=jnp.concatenate([zeros_half, peer_k2[i]], axis=1).astype(BF16),
            peer_u=peer_u[i].astype(BF16), peer_vt=peer_v[i].astype(BF16).T,
            norm_ple_g=row(norm_ple_g[i]), ple_w_gate=ple_w_gate[i].astype(BF16),
            ple_w_proj=ple_w_proj[i].astype(BF16),
        )
        layers.append((lw, lam_init))

    def run_group(x, p, ck, cv, cconv, cssm, q_off):
        k_rows, v_rows, conv_rows, ssm_rows = [], [], [], []
        for i, (lw, lam_init) in enumerate(layers):
            last = i == depth - 1
            x, k_new, v_new, conv_new, ssm_new = _layer(
                x, p[i],
                None if ck is None else ck[i], None if cv is None else cv[i],
                None if cconv is None else cconv[i], None if cssm is None else cssm[i],
                q_off, lw, cfg, lam_init, row(final_norm_g) if last else None)
            k_rows.append(k_new)
            v_rows.append(v_new)
            conv_rows.append(conv_new)
            ssm_rows.append(ssm_new)
        return x, jnp.stack(k_rows), jnp.stack(v_rows), jnp.stack(conv_rows), jnp.stack(ssm_rows)

    y_p, k_p, v_p, conv_p, ssm_p = run_group(x_prompt, p_prompt, None, None, None, None, 0)
    y_s, k_s, v_s, conv_s, ssm_s = run_group(x_sample, p_sample, cache_k, cache_v, state_conv, state_ssm,
                                             cache_k.shape[2])
    return (y_p, y_s, k_p, v_p, conv_p, ssm_p, k_s, v_s, conv_s, ssm_s)
```

```python
import functools
import math

import numpy as np
import jax
import jax.numpy as jnp
from jax import lax
from jax.experimental import pallas as pl
from jax.experimental.pallas import tpu as pltpu

F32 = jnp.float32
BF16 = jnp.bfloat16

EPS = 1e-6
CHUNK = 64
NEG_INF = -1e30
P_TOPK = 16

LANES = 128
SUBLANES = 8
VMEM_LIMIT = 52 * 1024 * 1024

ATT_TQ = 1024
ATT_TK = 1024
ATT_RQ = 512
ROW_TILE = 1024
SSD_CHUNK = 256
PEER_TT = 512

NT_DIMS = (((1,), (1,)), ((), ()))
TN_DIMS = (((0,), (0,)), ((), ()))


def _cparams(*sem, flags=None):
    return pltpu.CompilerParams(dimension_semantics=sem, vmem_limit_bytes=VMEM_LIMIT, flags=flags)


def _rms(x):
    return x * lax.rsqrt(jnp.mean(x * x, axis=-1, keepdims=True) + EPS)


def _split3(x):
    hi = x.astype(BF16)
    r1 = x - hi.astype(F32)
    mid = r1.astype(BF16)
    lo = (r1 - mid.astype(F32)).astype(BF16)
    return hi, mid, lo


def _rms_matmul_kernel(x_ref, g_ref, w_ref, *o_refs):
    h = (_rms(x_ref[...]) * g_ref[...]).astype(BF16)
    y = jnp.dot(h, w_ref[...], preferred_element_type=F32)
    for o_ref in o_refs:
        o_ref[...] = y.astype(o_ref.dtype)


def rms_matmul(x, g, w, out_dtypes, tm=ROW_TILE):
    n, d = x.shape
    c = w.shape[1]
    tm = min(tm, n)
    tn = min(c, 1024)
    assert n % tm == 0 and c % tn == 0
    outs = pl.pallas_call(
        _rms_matmul_kernel,
        grid=(c // tn, n // tm),
        in_specs=[pl.BlockSpec((tm, d), lambda j, i: (i, 0)),
                  pl.BlockSpec((1, d), lambda j, i: (0, 0)),
                  pl.BlockSpec((d, tn), lambda j, i: (0, j))],
        out_specs=[pl.BlockSpec((tm, tn), lambda j, i: (i, j)) for _ in out_dtypes],
        out_shape=[jax.ShapeDtypeStruct((n, c), dt) for dt in out_dtypes],
        compiler_params=_cparams("parallel", "parallel"),
        name="rms_matmul",
    )(x, g.reshape(1, d), w)
    return outs


def _attn_kernel(qi_tab, ki_tab, par_ref, q_ref, k_ref, v_ref, g_ref, o_ref,
                 m_scr, l_scr, acc_scr, *, tq, tk, rq, q_off, lk, n_heads, out_scale):
    h = pl.program_id(1)
    step = pl.program_id(2)
    qi = qi_tab[step]
    ki = ki_tab[step]
    slope = par_ref[h]
    lam = par_ref[n_heads]
    dk = LANES // 2

    qs = q_off + qi * tq
    ks = ki * tk
    last_vis = (((qs + tq - 1) // CHUNK) * CHUNK + CHUNK - 1) // tk
    k_last = jnp.minimum((lk - 1) // tk, last_vis)

    @pl.when(ki == 0)
    def _():
        m_scr[...] = jnp.full(m_scr.shape, NEG_INF, F32)
        l_scr[...] = jnp.zeros(l_scr.shape, F32)
        acc_scr[...] = jnp.zeros(acc_scr.shape, F32)

    k = k_ref[0]
    v = v_ref[0]
    lane = lax.broadcasted_iota(jnp.int32, (rq, LANES), 1)

    def process(r, bias_fn):
        rows = slice(r * rq, (r + 1) * rq)
        q = q_ref[0, rows, :]
        zero = jnp.zeros_like(q)
        qq = jnp.concatenate([jnp.where(lane < dk, q, zero), jnp.where(lane >= dk, q, zero)], axis=0)
        s = lax.dot_general(qq, k, NT_DIMS, preferred_element_type=F32)
        bm = bias_fn(r)
        ps, alphas = [], []
        for j in range(2):
            sj = s[j * rq:(j + 1) * rq] + bm
            m_old = m_scr[j, rows, :]
            m_new = jnp.maximum(m_old, jnp.max(sj, axis=-1, keepdims=True))
            alpha = jnp.exp(m_old - m_new)
            p = jnp.exp(sj - jnp.tile(m_new, (1, tk // LANES)))
            l_scr[j, rows, :] = alpha * l_scr[j, rows, :] + jnp.sum(p, axis=-1, keepdims=True)
            m_scr[j, rows, :] = m_new
            alphas.append(alpha)
            ps.append(p.astype(BF16))
        pv = jnp.dot(jnp.concatenate(ps, axis=0), v, preferred_element_type=F32)
        for j in range(2):
            acc_scr[j, rows, :] = alphas[j] * acc_scr[j, rows, :] + pv[j * rq:(j + 1) * rq]

    def bias_row(r):
        kpos = ks + lax.broadcasted_iota(jnp.int32, (1, tk), 1)
        return slope * (kpos - qs).astype(F32)

    def bias_full(r):
        ii = r * rq + lax.broadcasted_iota(jnp.int32, (rq, tk), 0)
        qpos = qs + ii
        kpos = ks + lax.broadcasted_iota(jnp.int32, (rq, tk), 1)
        dist = jnp.abs(qpos - kpos)
        visible = (lax.shift_right_logical(kpos, 6) <= lax.shift_right_logical(qpos, 6)) & (kpos < lk)
        return jnp.where(visible, slope * (ii - dist).astype(F32), NEG_INF)

    n_sub = tq // rq
    simple = (ks + tk - 1 <= qs) & (ks + tk <= lk)

    @pl.when(simple)
    def _():
        for r in range(n_sub):
            process(r, bias_row)

    @pl.when(jnp.logical_not(simple) & (ki <= k_last))
    def _():
        for r in range(n_sub):
            process(r, bias_full)

    @pl.when(ki == k_last)
    def _():
        o = acc_scr[0] / l_scr[0] - lam * (acc_scr[1] / l_scr[1])
        o = _rms(o) * g_ref[...] * out_scale
        o_ref[0] = o.astype(o_ref.dtype)


def diff_attention(q, k, v, par, sub_g, *, lq, lk, q_off, tq, tk, rq, n_heads, out_scale):
    bsz = q.shape[0]
    lk_pad = k.shape[1]
    assert lq % tq == 0 and lk_pad % tk == 0 and tq % rq == 0 and CHUNK == 64
    nq, nk = lq // tq, lk_pad // tk
    pairs = []
    for qi in range(nq):
        qs = q_off + qi * tq
        k_last = min((lk - 1) // tk, (((qs + tq - 1) // CHUNK) * CHUNK + CHUNK - 1) // tk)
        pairs += [(qi, ki) for ki in range(k_last + 1)]
    qi_tab = jnp.asarray(np.array([p[0] for p in pairs], np.int32))
    ki_tab = jnp.asarray(np.array([p[1] for p in pairs], np.int32))
    kern = functools.partial(_attn_kernel, tq=tq, tk=tk, rq=rq, q_off=q_off, lk=lk,
                             n_heads=n_heads, out_scale=out_scale)
    grid_spec = pltpu.PrefetchScalarGridSpec(
        num_scalar_prefetch=2,
        grid=(bsz, n_heads, len(pairs)),
        in_specs=[pl.BlockSpec(memory_space=pltpu.SMEM),
                  pl.BlockSpec((1, tq, LANES), lambda b, h, s, qt, kt: (b, qt[s], h)),
                  pl.BlockSpec((1, tk, LANES), lambda b, h, s, qt, kt: (b, kt[s], h)),
                  pl.BlockSpec((1, tk, LANES), lambda b, h, s, qt, kt: (b, kt[s], h)),
                  pl.BlockSpec((1, LANES), lambda b, h, s, qt, kt: (0, 0))],
        out_specs=pl.BlockSpec((1, tq, LANES), lambda b, h, s, qt, kt: (b, qt[s], h)),
        scratch_shapes=[pltpu.VMEM((2, tq, LANES), F32), pltpu.VMEM((2, tq, LANES), F32),
                        pltpu.VMEM((2, tq, LANES), F32)])
    return pl.pallas_call(
        kern, grid_spec=grid_spec,
        out_shape=jax.ShapeDtypeStruct((bsz, lq, n_heads * LANES), BF16),
        compiler_params=_cparams("parallel", "parallel", "arbitrary"),
        name="diff_attention",
    )(qi_tab, ki_tab, par, q, k, v, sub_g.reshape(1, LANES))


def _ssm_kernel(z_ref, x_ref, b_ref, c_ref, dt_ref, dtb_ref, alog_ref, dsk_ref,
                cwx_ref, cwb_ref, cwc_ref, cbx_ref, cbb_ref, cbc_ref,
                px_ref, pb_ref, pc_ref, h0_ref, ng_ref,
                yg_ref, hfin_ref, ht_scr, xpx, xpb, xpc, *, lc, l_valid, n_conv):
    c = pl.program_id(2)
    nc = pl.num_programs(2)
    npair = ht_scr.shape[0]
    pad = SUBLANES
    nh = n_conv - 1

    @pl.when(c == 0)
    def _():
        for pr in range(npair):
            ht_scr[pr] = h0_ref[0, pr].T
        xpx[pad - nh:pad, :] = px_ref[0]
        xpb[pad - nh:pad, :] = pb_ref[0]
        xpc[pad - nh:pad, :] = pc_ref[0]

    @pl.when(c > 0)
    def _():
        xpx[pad - nh:pad, :] = xpx[pad + lc - nh:pad + lc, :]
        xpb[pad - nh:pad, :] = xpb[pad + lc - nh:pad + lc, :]
        xpc[pad - nh:pad, :] = xpc[pad + lc - nh:pad + lc, :]

    xpx[pad:pad + lc, :] = x_ref[0]
    xpb[pad:pad + lc, :] = b_ref[0]
    xpc[pad:pad + lc, :] = c_ref[0]

    def conv_silu(xp, cw_ref, cb_ref):
        acc = cb_ref[...]
        for w in range(n_conv):
            acc = acc + cw_ref[w:w + 1, :] * xp[pad - nh + w:pad - nh + w + lc, :]
        return acc * jax.nn.sigmoid(acc)

    xs = conv_silu(xpx, cwx_ref, cbx_ref)
    bm = conv_silu(xpb, cwb_ref, cbb_ref).astype(BF16)
    cm = conv_silu(xpc, cwc_ref, cbc_ref).astype(BF16)

    t_col = c * lc + lax.broadcasted_iota(jnp.int32, (lc, LANES), 0)
    dt_c = jax.nn.softplus(dt_ref[0] + dtb_ref[0])
    dt_c = jnp.where(t_col < l_valid, dt_c, 0.0)
    da_c = dt_c * (-jnp.exp(alog_ref[0]))
    ri = lax.broadcasted_iota(jnp.int32, (lc, lc), 0)
    ci = lax.broadcasted_iota(jnp.int32, (lc, lc), 1)
    causal = ri >= ci
    tril = jnp.where(causal, 1.0, 0.0).astype(BF16)
    acum_c = None
    for piece in _split3(da_c):
        t = jnp.dot(tril, piece, preferred_element_type=F32)
        acum_c = t if acum_c is None else acum_c + t
    acum_r = acum_c.T
    dt_r = dt_c.T
    a_last = acum_c[lc - 1:lc, :]
    wdec_c = jnp.exp(a_last - acum_c) * dt_c
    eac_c = jnp.exp(acum_c)
    ea_last = jnp.exp(a_last)

    cb = lax.dot_general(cm, bm, NT_DIMS, preferred_element_type=F32)
    lane = lax.broadcasted_iota(jnp.int32, (lc, LANES), 1)
    lane1 = lax.broadcasted_iota(jnp.int32, (1, LANES), 1)
    half = LANES // 2
    ys = []
    for pr in range(npair):
        xp_f = xs[:, pr * LANES:(pr + 1) * LANES]
        y = dsk_ref[:, pr * LANES:(pr + 1) * LANES] * xp_f
        for e in range(2):
            hh = 2 * pr + e
            seg = acum_c[:, hh:hh + 1] - acum_r[hh:hh + 1, :]
            lmat = jnp.exp(jnp.where(causal, seg, NEG_INF))
            m = (cb * lmat * dt_r[hh:hh + 1, :]).astype(BF16)
            sel = (lane < half) if e == 0 else (lane >= half)
            xh = jnp.where(sel, xp_f, 0.0).astype(BF16)
            y = y + jnp.dot(m, xh, preferred_element_type=F32)
        h_a, h_b = 2 * pr, 2 * pr + 1
        scale = jnp.where(lane < half, eac_c[:, h_a:h_a + 1], eac_c[:, h_b:h_b + 1])
        ht = ht_scr[pr]
        y = y + jnp.dot(cm, ht.astype(BF16), preferred_element_type=F32) * scale
        wd = jnp.where(lane < half, wdec_c[:, h_a:h_a + 1], wdec_c[:, h_b:h_b + 1])
        dec = jnp.where(lane1 < half, ea_last[:, h_a:h_a + 1], ea_last[:, h_b:h_b + 1])
        upd = lax.dot_general(bm, (wd * xp_f).astype(BF16), TN_DIMS, preferred_element_type=F32)
        ht_scr[pr] = dec * ht + upd
        ys.append(y)
    y = jnp.concatenate(ys, axis=-1)

    zz = z_ref[0]
    yg = y * (zz * jax.nn.sigmoid(zz))
    yg_ref[0] = (_rms(yg) * ng_ref[...]).astype(yg_ref.dtype)

    @pl.when(c == nc - 1)
    def _():
        for pr in range(npair):
            hfin_ref[0, pr] = ht_scr[pr].T


def ssm_branch(z, xbc, dt_pad, conv_past, h0, conv_w, conv_b, dtb_pad, alog_pad, dsk_exp, norm_g,
               *, lc, l_valid, n_groups, d_inner, d_state):
    bsz, length, _ = z.shape
    assert length % lc == 0 and lc % LANES == 0
    nc = length // lc
    gw = d_inner // n_groups
    assert gw % LANES == 0 and d_state == LANES
    npair = gw // LANES
    n_conv = conv_w.shape[0]
    xb0 = d_inner // d_state
    cb0 = xb0 + n_groups
    kern = functools.partial(_ssm_kernel, lc=lc, l_valid=l_valid, n_conv=n_conv)
    seq = lambda width, off: pl.BlockSpec((1, lc, width), lambda b, g, c: (b, c, off + g))
    par = lambda rows, width, off: pl.BlockSpec((rows, width), lambda b, g, c: (0, off + g))
    hist = lambda width, off: pl.BlockSpec((1, n_conv - 1, width), lambda b, g, c: (b, 0, off + g))
    state_spec = pl.BlockSpec((1, npair, LANES, d_state), lambda b, g, c: (b, g, 0, 0))
    yg, hfin = pl.pallas_call(
        kern,
        grid=(bsz, n_groups, nc),
        in_specs=[seq(gw, 0), seq(gw, 0), seq(d_state, xb0), seq(d_state, cb0), seq(LANES, 0),
                  pl.BlockSpec((1, 1, LANES), lambda b, g, c: (g, 0, 0)),
                  pl.BlockSpec((1, 1, LANES), lambda b, g, c: (g, 0, 0)),
                  par(1, gw, 0),
                  par(n_conv, gw, 0), par(n_conv, d_state, xb0), par(n_conv, d_state, cb0),
                  par(1, gw, 0), par(1, d_state, xb0), par(1, d_state, cb0),
                  hist(gw, 0), hist(d_state, xb0), hist(d_state, cb0),
                  state_spec, par(1, gw, 0)],
        out_specs=[seq(gw, 0), state_spec],
        out_shape=[jax.ShapeDtypeStruct((bsz, length, d_inner), BF16),
                   jax.ShapeDtypeStruct(h0.shape, F32)],
        scratch_shapes=[pltpu.VMEM((npair, d_state, LANES), F32),
                        pltpu.VMEM((lc + 2 * SUBLANES, gw), F32),
                        pltpu.VMEM((lc + 2 * SUBLANES, d_state), F32),
                        pltpu.VMEM((lc + 2 * SUBLANES, d_state), F32)],
        compiler_params=_cparams("parallel", "parallel", "arbitrary"),
        name="ssm_branch",
    )(z, xbc, xbc, xbc, dt_pad, dtb_pad, alog_pad, dsk_exp,
      conv_w, conv_w, conv_w, conv_b, conv_b, conv_b,
      conv_past, conv_past, conv_past, h0, norm_g)
    return yg, hfin


def _mix_kernel(x_ref, a_ref, y_ref, ga_ref, gs_ref, wa_ref, ws_ref, wo_ref, o_ref):
    o_a = jnp.dot(a_ref[...], wa_ref[...], preferred_element_type=F32)
    o_s = jnp.dot(y_ref[...], ws_ref[...], preferred_element_type=F32)
    mix = jax.nn.sigmoid(ga_ref[...]) * o_a + jax.nn.sigmoid(gs_ref[...]) * o_s
    o_ref[...] = x_ref[...] + jnp.dot(mix.astype(BF16), wo_ref[...], preferred_element_type=F32)


def mixer_out(x, attn, yg, gates, wa, ws, wo, tm=512):
    n, d = x.shape
    tm = min(tm, n)
    assert n % tm == 0 and gates.shape[1] == 2 * d
    row = lambda width, j=0: pl.BlockSpec((tm, width), lambda i, j=j: (i, j))
    full = lambda a: pl.BlockSpec(a.shape, lambda i: (0, 0))
    return pl.pallas_call(
        _mix_kernel,
        grid=(n // tm,),
        in_specs=[row(d), row(attn.shape[1]), row(yg.shape[1]), row(d, 0), row(d, 1),
                  full(wa), full(ws), full(wo)],
        out_specs=row(d),
        out_shape=jax.ShapeDtypeStruct((n, d), F32),
        compiler_params=_cparams("parallel"),
        name="mixer_out",
    )(x, attn, yg, gates, gates, wa, ws, wo)


def _staircase_pairs(k):
    return [(i, j) for i in range(k) for j in range(k) if (i + 1) * (j + 1) <= k]


def _oddeven_merge(lo, hi, r):
    step = r * 2
    if step < hi - lo:
        yield from _oddeven_merge(lo, hi, step)
        yield from _oddeven_merge(lo + r, hi, step)
        yield from [(i, i + r) for i in range(lo + r, hi - r, step)]
    else:
        yield (lo, lo + r)


def _oddeven_sort_pairs(lo, hi):
    if hi - lo >= 1:
        mid = lo + (hi - lo) // 2
        yield from _oddeven_sort_pairs(lo, mid)
        yield from _oddeven_sort_pairs(mid + 1, hi)
        yield from _oddeven_merge(lo, hi, 1)


def _sort_desc(vals):
    vals = list(vals)
    for i, j in _oddeven_sort_pairs(0, len(vals) - 1):
        vals[i], vals[j] = jnp.maximum(vals[i], vals[j]), jnp.minimum(vals[i], vals[j])
    return vals


def _bitonic_merge_desc(vals):
    vals = list(vals)
    n = len(vals)
    d = n // 2
    while d >= 1:
        for i in range(n):
            if i & d == 0:
                vals[i], vals[i + d] = jnp.maximum(vals[i], vals[i + d]), jnp.minimum(vals[i], vals[i + d])
        d //= 2
    return vals


def _merge_sublanes_top(vals):
    n = len(vals)
    shift = SUBLANES // 2
    while shift >= 1:
        other = [pltpu.roll(v, shift, 0) for v in vals]
        vals = _bitonic_merge_desc([jnp.maximum(vals[i], other[n - 1 - i]) for i in range(n)])
        shift //= 2
    return vals


def _router_kernel(x_ref, g_ref, wqt_ref, k1_ref, k2_ref,
                   h_ref, n1_ref, e1_ref, rk2_ref, e2_ref, qt_scr, *, n_heads, pairs):
    tt = x_ref.shape[0]
    nk = k1_ref.shape[0]
    h = (_rms(x_ref[...]) * g_ref[...]).astype(BF16)
    h_ref[...] = h
    qt_scr[...] = lax.dot_general(wqt_ref[...], h, NT_DIMS, preferred_element_type=F32).astype(BF16)
    sub = lax.broadcasted_iota(jnp.int32, (SUBLANES, LANES), 0)
    n_cv = -(-len(pairs) // SUBLANES)
    n_cv_pad = pl.next_power_of_2(n_cv)

    n_rows = nk // SUBLANES

    def head_tile(s1, s2, ls):
        s1cols = [s1[r * SUBLANES:(r + 1) * SUBLANES, ls] for r in range(n_rows)]
        s2cols = [s2[r * SUBLANES:(r + 1) * SUBLANES, ls] for r in range(n_rows)]
        v1 = _merge_sublanes_top(_sort_desc(s1cols))[:P_TOPK]
        v2 = _merge_sublanes_top(_sort_desc(s2cols))[:P_TOPK]
        sums = {(i, j): v1[i] + v2[j] for (i, j) in pairs}
        cvs = []
        for c in range(n_cv_pad):
            cv = jnp.full((SUBLANES, LANES), -jnp.inf, F32)
            for k in range(SUBLANES):
                pidx = c * SUBLANES + k
                if pidx < len(pairs):
                    cv = jnp.where(sub == k, sums[pairs[pidx]], cv)
            cvs.append(cv)
        top = _merge_candidates(_sort_desc(cvs))
        tau = top[P_TOPK - 1]
        zsum = jnp.zeros((SUBLANES, LANES), F32)
        for r in range(P_TOPK):
            zsum = zsum + jnp.exp(top[r] - top[0])
        cnts = []
        for i in range(P_TOPK):
            cnt = jnp.zeros((SUBLANES, LANES), F32)
            for (pi, pj) in pairs:
                if pi == i:
                    cnt = jnp.where(sums[(pi, pj)] >= tau, float(pj + 1), cnt)
            cnts.append(cnt)
        n1, e1, rk2, e2 = [], [], [], []
        for r in range(n_rows):
            nrow = jnp.zeros((SUBLANES, LANES), F32)
            rank = jnp.full((SUBLANES, LANES), float(P_TOPK), F32)
            for i in range(P_TOPK):
                nrow = jnp.where(s1cols[r] == v1[i], cnts[i], nrow)
            for j in reversed(range(P_TOPK)):
                rank = jnp.where(v2[j] <= s2cols[r], float(j), rank)
            n1.append(nrow)
            rk2.append(rank)
            e1.append(jnp.exp(s1cols[r] - v1[0]))
            e2.append(jnp.exp(s2cols[r] - v2[0]) / zsum)
        return n1, e1, rk2, e2

    def pack_pair(lo, hi):
        lo_w = pltpu.bitcast(lo.astype(BF16).astype(F32), jnp.uint32)
        hi_w = pltpu.bitcast(hi.astype(BF16).astype(F32), jnp.uint32)
        return hi_w | lax.shift_right_logical(lo_w, jnp.uint32(16))

    half = n_heads // 2

    def head_pair(hp, carry):
        scores = []
        for hd in (hp, hp + half):
            qh = qt_scr[pl.ds(pl.multiple_of(hd * nk, nk), nk), :]
            scores.append((jnp.dot(k1_ref[...], qh, preferred_element_type=F32),
                           jnp.dot(k2_ref[...], qh, preferred_element_type=F32)))
        for lg in range(tt // LANES):
            ls = slice(lg * LANES, (lg + 1) * LANES)
            lo = head_tile(scores[0][0], scores[0][1], ls)
            hi = head_tile(scores[1][0], scores[1][1], ls)
            for out_ref, lo_l, hi_l in zip((n1_ref, e1_ref, rk2_ref, e2_ref), lo, hi):
                for r in range(n_rows):
                    out_ref[0, hp, r * SUBLANES:(r + 1) * SUBLANES, ls] = pack_pair(lo_l[r], hi_l[r])
        return carry

    lax.fori_loop(0, half, head_pair, 0)


def _merge_candidates(cvs):
    assert len(cvs) == SUBLANES and P_TOPK == 2 * SUBLANES
    other = [pltpu.roll(v, SUBLANES // 2, 0) for v in cvs]
    vals = _bitonic_merge_desc(cvs + other[::-1])
    for shift in (SUBLANES // 4, SUBLANES // 8):
        oth = [pltpu.roll(v, shift, 0) for v in vals]
        vals = _bitonic_merge_desc([jnp.maximum(vals[i], oth[P_TOPK - 1 - i]) for i in range(P_TOPK)])
    return vals


def peer_router(x, g, wqt, k1p, k2p, *, n_heads, pairs, tt):
    n, d = x.shape
    assert n % tt == 0 and tt % LANES == 0
    nt = n // tt
    nk = k1p.shape[0]
    kern = functools.partial(_router_kernel, n_heads=n_heads, pairs=pairs)
    full = lambda a: pl.BlockSpec(a.shape, lambda i: (0,) * a.ndim)
    assert n_heads % 2 == 0
    tspec = pl.BlockSpec((1, n_heads // 2, nk, tt), lambda i: (i, 0, 0, 0))
    tshape = jax.ShapeDtypeStruct((nt, n_heads // 2, nk, tt), jnp.uint32)
    return pl.pallas_call(
        kern,
        grid=(nt,),
        in_specs=[pl.BlockSpec((tt, d), lambda i: (i, 0)), full(g), full(wqt), full(k1p), full(k2p)],
        out_specs=[pl.BlockSpec((tt, d), lambda i: (i, 0)), tspec, tspec, tspec, tspec],
        out_shape=[jax.ShapeDtypeStruct((n, d), BF16), tshape, tshape, tshape, tshape],
        scratch_shapes=[pltpu.VMEM((wqt.shape[0], tt), BF16)],
        compiler_params=_cparams("parallel"),
        name="peer_router",
    )(x, g, wqt, k1p, k2p)


def _gelu_tanh(x):
    c = math.sqrt(2.0 / math.pi)
    return x * (0.5 * (1.0 + jnp.tanh(c * (x + 0.044715 * (x * x * x)))))


def _experts_kernel(x_ref, h_ref, n1r_ref, e1r_ref, rk2_ref, e2_ref, u_ref, vt_ref,
                    o_ref, acc_scr, a_scr, *, rows_per_step, bh):
    s = pl.program_id(1)
    n_pairs, nk = rk2_ref.shape[1], rk2_ref.shape[2]
    tt = h_ref.shape[0]
    as_bf16 = lambda w: pltpu.bitcast(w, BF16)

    @pl.when(s == 0)
    def _():
        acc_scr[...] = jnp.zeros(acc_scr.shape, F32)

    pt = lax.dot_general(u_ref[...], h_ref[...], NT_DIMS, preferred_element_type=F32)
    for a in range(rows_per_step):
        for lg in range(tt // LANES):
            ls = slice(lg * LANES, (lg + 1) * LANES)
            for b0 in range(0, nk, bh):
                gsum = jnp.zeros((2 * bh, LANES), BF16)
                for hp in range(n_pairs):
                    n1 = as_bf16(jnp.broadcast_to(n1r_ref[0, hp, a:a + 1, ls], (bh, LANES)))
                    e1 = as_bf16(jnp.broadcast_to(e1r_ref[0, hp, a:a + 1, ls], (bh, LANES)))
                    keep = as_bf16(rk2_ref[0, hp, b0:b0 + bh, ls]) < n1
                    w = e1 * as_bf16(e2_ref[0, hp, b0:b0 + bh, ls])
                    gsum = gsum + jnp.where(keep, w, jnp.zeros_like(w))
                gw = pltpu.bitcast(gsum, jnp.uint32)
                gate = (pltpu.bitcast(lax.shift_left(gw, jnp.uint32(16)), F32)
                        + pltpu.bitcast(gw & jnp.uint32(0xFFFF0000), F32))
                r0 = a * nk + b0
                a_scr[r0:r0 + bh, ls] = (_gelu_tanh(pt[r0:r0 + bh, ls]) * gate).astype(BF16)
    acc_scr[...] += jnp.dot(vt_ref[...], a_scr[...], preferred_element_type=F32)

    @pl.when(s == pl.num_programs(1) - 1)
    def _():
        o_ref[...] = x_ref[...] + acc_scr[...].T


def peer_experts(x, h, n1, e1, rk2, e2, u, vt):
    n, d = x.shape
    nt, n_pairs, nk, tt = rk2.shape
    n_exp = u.shape[0]
    rows_per_step = SUBLANES
    bh = 8 * SUBLANES
    eb = rows_per_step * nk
    assert n_exp % eb == 0 and n == nt * tt and nk % bh == 0
    kern = functools.partial(_experts_kernel, rows_per_step=rows_per_step, bh=bh)
    tspec = pl.BlockSpec((1, n_pairs, nk, tt), lambda i, s: (i, 0, 0, 0))
    rspec = pl.BlockSpec((1, n_pairs, rows_per_step, tt), lambda i, s: (i, 0, s, 0))
    return pl.pallas_call(
        kern,
        grid=(nt, n_exp // eb),
        in_specs=[pl.BlockSpec((tt, d), lambda i, s: (i, 0)),
                  pl.BlockSpec((tt, d), lambda i, s: (i, 0)),
                  rspec, rspec, tspec, tspec,
                  pl.BlockSpec((eb, d), lambda i, s: (s, 0)),
                  pl.BlockSpec((d, eb), lambda i, s: (0, s))],
        out_specs=pl.BlockSpec((tt, d), lambda i, s: (i, 0)),
        out_shape=jax.ShapeDtypeStruct((n, d), F32),
        scratch_shapes=[pltpu.VMEM((d, tt), F32), pltpu.VMEM((eb, tt), BF16)],
        compiler_params=_cparams("parallel", "arbitrary"),
        name="peer_experts",
    )(x, h, n1, e1, rk2, e2, u, vt)


def _ple_kernel(x_ref, p_ref, g_ref, wg_ref, wp_ref, gf_ref, o_ref, *, final):
    x = x_ref[...]
    hg = (_rms(x) * g_ref[...]).astype(BF16)
    gate = jax.nn.sigmoid(jnp.dot(hg, wg_ref[...], preferred_element_type=F32))
    pe = jnp.dot(p_ref[...].astype(BF16), wp_ref[...], preferred_element_type=F32)
    x = x + gate * pe
    if final:
        x = _rms(x) * gf_ref[...]
    o_ref[...] = x


def ple_out(x, p, g, wg, wp, gf, *, final, tm=512):
    n, d = x.shape
    tm = min(tm, n)
    assert n % tm == 0
    full = lambda a: pl.BlockSpec(a.shape, lambda i: (0, 0))
    return pl.pallas_call(
        functools.partial(_ple_kernel, final=final),
        grid=(n // tm,),
        in_specs=[pl.BlockSpec((tm, d), lambda i: (i, 0)), pl.BlockSpec((tm, p.shape[1]), lambda i: (i, 0)),
                  full(g), full(wg), full(wp), full(gf)],
        out_specs=pl.BlockSpec((tm, d), lambda i: (i, 0)),
        out_shape=jax.ShapeDtypeStruct((n, d), F32),
        compiler_params=_cparams("parallel"),
        name="ple_out",
    )(x, p, g, wg, wp, gf)


def _round_up(x, m):
    return -(-x // m) * m


def _layer(x, p_i, k_past, v_past, conv_past, ssm_past, q_off, lw, cfg, lam_init, final_g):
    bsz, length, d = x.shape
    n = bsz * length
    n_ah, a_dk, a_dv = cfg["a_heads"], cfg["a_dk"], cfg["a_dv"]
    d_inner, n_groups, d_state = cfg["d_inner"], cfg["s_groups"], cfg["d_state"]
    x2 = x.reshape(n, d)

    g_mix = lw["norm_mix_g"]
    (q,) = rms_matmul(x2, g_mix, lw["w_q"], [BF16])
    k_f32, k_bf = rms_matmul(x2, g_mix, lw["w_k"], [F32, BF16])
    v_f32, v_bf = rms_matmul(x2, g_mix, lw["w_v"], [F32, BF16])
    (z,) = rms_matmul(x2, g_mix, lw["w_z"], [F32])
    (xbc,) = rms_matmul(x2, g_mix, lw["w_xbc"], [F32])
    (dt_pad,) = rms_matmul(x2, g_mix, lw["w_dt"], [F32])
    (gates,) = rms_matmul(x2, g_mix, lw["w_gate"], [F32])

    hw = n_ah * LANES
    q3 = q.reshape(bsz, length, hw)
    k3 = k_bf.reshape(bsz, length, hw)
    v3 = v_bf.reshape(bsz, length, hw)
    if k_past is None:
        lk = length
        tq, tk = min(ATT_TQ, length), min(ATT_TK, length)
        rq = min(ATT_RQ, tq)
    else:
        past = k_past.shape[1]
        lk = past + length
        lk_pad = _round_up(lk, LANES)
        zpad = jnp.zeros((bsz, lk_pad - lk, hw), BF16)
        k3 = jnp.concatenate([k_past.reshape(bsz, past, hw).astype(BF16), k3, zpad], axis=1)
        v3 = jnp.concatenate([v_past.reshape(bsz, past, hw).astype(BF16), v3, zpad], axis=1)
        tq, tk, rq = length, lk_pad, length
    attn = diff_attention(q3, k3, v3, lw["attn_par"], lw["attn_subln_g"], lq=length, lk=lk, q_off=q_off,
                          tq=tq, tk=tk, rq=rq, n_heads=n_ah, out_scale=1.0 - lam_init)

    lc = min(SSD_CHUNK, _round_up(length, LANES))
    l_pad = _round_up(length, lc)
    z3 = z.reshape(bsz, length, d_inner)
    xbc3 = xbc.reshape(bsz, length, -1)
    dt3 = dt_pad.reshape(bsz, length, -1)
    if l_pad != length:
        padt = lambda a: jnp.pad(a, ((0, 0), (0, l_pad - length), (0, 0)))
        z3p, xbc3p, dt3p = padt(z3), padt(xbc3), padt(dt3)
    else:
        z3p, xbc3p, dt3p = z3, xbc3, dt3
    n_conv = lw["conv_w"].shape[0]
    if conv_past is None:
        conv_past = jnp.zeros((bsz, n_conv - 1, xbc3.shape[-1]), F32)
    if ssm_past is None:
        ssm_past = jnp.zeros((bsz, cfg["s_heads"], cfg["s_head_p"], d_state), F32)
    h0 = ssm_past.astype(F32).reshape(bsz, -1, LANES, d_state)
    yg, hfin = ssm_branch(z3p, xbc3p, dt3p, conv_past.astype(F32), h0, lw["conv_w"], lw["conv_b"],
                          lw["dtb_pad"], lw["alog_pad"], lw["dsk_exp"], lw["ssm_norm_g"],
                          lc=lc, l_valid=length, n_groups=n_groups, d_inner=d_inner, d_state=d_state)
    yg = yg[:, :length].reshape(n, d_inner)
    ssm_new = hfin.reshape(ssm_past.shape)
    if length >= n_conv - 1:
        conv_new = xbc3[:, length - (n_conv - 1):]
    else:
        conv_new = jnp.concatenate([conv_past.astype(F32), xbc3], axis=1)[:, -(n_conv - 1):]

    x2 = mixer_out(x2, attn.reshape(n, hw), yg, gates, lw["w_attn_o"], lw["w_ssm_o"], lw["w_out"])

    tt = min(PEER_TT, n)
    hp, n1, e1, rk2, e2 = peer_router(
        x2, lw["norm_ffn_g"], lw["peer_wqt"], lw["k1p"], lw["k2p"],
        n_heads=cfg["p_heads"], pairs=cfg["pairs"], tt=tt)
    x2 = peer_experts(x2, hp, n1, e1, rk2, e2, lw["peer_u"], lw["peer_vt"])

    x2 = ple_out(x2, p_i.reshape(n, -1), lw["norm_ple_g"], lw["ple_w_gate"], lw["ple_w_proj"],
                 lw["norm_ple_g"] if final_g is None else final_g, final=final_g is not None)

    k_new = k_f32.reshape(bsz, length, n_ah, 2, a_dk)
    v_new = v_f32.reshape(bsz, length, n_ah, a_dv)
    return x2.reshape(bsz, length, d), k_new, v_new, conv_new, ssm_new


def kernel(x_prompt, x_sample, p_prompt, p_sample, cache_k, cache_v, state_conv, state_ssm, norm_mix_g, w_in, lam_q1, lam_k1, lam_q2, lam_k2, attn_subln_g, w_attn_o, conv_w, conv_b, dt_bias, a_log, d_skip, ssm_norm_g, w_ssm_o, w_out, norm_ffn_g, peer_wq, peer_k1, peer_k2, peer_u, peer_v, norm_ple_g, ple_w_gate, ple_w_proj, final_norm_g):
    depth = w_in.shape[0]
    d = x_prompt.shape[-1]
    a_heads, a_dk, a_dv = cache_k.shape[3], cache_k.shape[5], cache_v.shape[4]
    s_heads, s_head_p, d_state = state_ssm.shape[2:]
    d_inner = ssm_norm_g.shape[1]
    conv_dim = conv_w.shape[2]
    s_groups = (conv_dim - d_inner) // (2 * d_state)
    hpg = s_heads // s_groups
    p_nkeys, p_half = peer_k1.shape[1], peer_k1.shape[2]
    p_heads = peer_wq.shape[2] // (2 * p_half)
    assert 2 * a_dk == LANES and a_dv == LANES and 2 * s_head_p == LANES and d_state == LANES
    assert p_nkeys == LANES and 2 * p_half == LANES and hpg <= LANES
    pairs = _staircase_pairs(P_TOPK)
    assert len(pairs) <= SUBLANES * SUBLANES
    cfg = dict(a_heads=a_heads, a_dk=a_dk, a_dv=a_dv, d_inner=d_inner, s_groups=s_groups, d_state=d_state,
               s_heads=s_heads, s_head_p=s_head_p, p_heads=p_heads, pairs=pairs)

    c_q = a_heads * 2 * a_dk
    c_v = a_heads * a_dv
    splits = np.cumsum([0, c_q, c_q, c_v, d_inner, conv_dim, s_heads, 2 * d])
    row = lambda a: a.astype(F32).reshape(1, -1)

    layers = []
    for i in range(depth):
        w = w_in[i]
        sl = lambda j: w[:, splits[j]:splits[j + 1]]
        w_dt = jnp.zeros((d, s_groups, LANES), F32).at[:, :, :hpg].set(sl(5).reshape(d, s_groups, hpg))
        grp = lambda a, fill: jnp.full((s_groups, 1, LANES), fill, F32).at[:, 0, :hpg].set(
            a.astype(F32).reshape(s_groups, hpg))
        f32 = jnp.float32
        lam_init = 0.8 - 0.6 * math.exp(-0.3 * i)
        lam = (jnp.exp(jnp.sum(lam_q1[i].astype(f32) * lam_k1[i].astype(f32)))
               - jnp.exp(jnp.sum(lam_q2[i].astype(f32) * lam_k2[i].astype(f32))) + lam_init)
        slopes = 2.0 ** (-8.0 * (jnp.arange(a_heads, dtype=f32) + 1.0) / a_heads)
        zeros_half = jnp.zeros((p_nkeys, p_half), F32)
        lw = dict(
            norm_mix_g=norm_mix_g[i],
            w_q=(sl(0) * (a_dk ** -0.5)).astype(BF16), w_k=sl(1).astype(BF16), w_v=sl(2).astype(BF16),
            w_z=sl(3).astype(BF16), w_xbc=sl(4).astype(BF16), w_gate=sl(6).astype(BF16),
            w_dt=w_dt.reshape(d, s_groups * LANES).astype(BF16),
            attn_par=jnp.concatenate([slopes, lam.reshape(1)]).astype(F32),
            attn_subln_g=attn_subln_g[i].astype(F32),
            w_attn_o=w_attn_o[i].astype(BF16), w_ssm_o=w_ssm_o[i].astype(BF16), w_out=w_out[i].astype(BF16),
            conv_w=conv_w[i].astype(F32), conv_b=row(conv_b[i]),
            dtb_pad=grp(dt_bias[i], 0.0), alog_pad=grp(a_log[i], 0.0),
            dsk_exp=row(jnp.repeat(d_skip[i], s_head_p)), ssm_norm_g=row(ssm_norm_g[i]),
            norm_ffn_g=row(norm_ffn_g[i]), peer_wqt=peer_wq[i].astype(BF16).T,
            k1p=jnp.concatenate([peer_k1[i], zeros_half], axis=1).astype(BF16),
            k2p=jnp.concatenate([zeros_half, peer_k2[i]], axis=1).astype(BF16),
            peer_u=peer_u[i].astype(BF16), peer_vt=peer_v[i].astype(BF16).T,
            norm_ple_g=row(norm_ple_g[i]), ple_w_gate=ple_w_gate[i].astype(BF16),
            ple_w_proj=ple_w_proj[i].astype(BF16),
        )
        layers.append((lw, lam_init))

    def run_group(x, p, ck, cv, cconv, cssm, q_off):
        k_rows, v_rows, conv_rows, ssm_rows = [], [], [], []
        for i, (lw, lam_init) in enumerate(layers):
            last = i == depth - 1
            x, k_new, v_new, conv_new, ssm_new = _layer(
                x, p[i],
                None if ck is None else ck[i], None if cv is None else cv[i],
                None if cconv is None else cconv[i], None if cssm is None else cssm[i],
                q_off, lw, cfg, lam_init, row(final_norm_g) if last else None)
            k_rows.append(k_new)
            v_rows.append(v_new)
            conv_rows.append(conv_new)
            ssm_rows.append(ssm_new)
        return x, jnp.stack(k_rows), jnp.stack(v_rows), jnp.stack(conv_rows), jnp.stack(ssm_rows)

    y_p, k_p, v_p, conv_p, ssm_p = run_group(x_prompt, p_prompt, None, None, None, None, 0)
    y_s, k_s, v_s, conv_s, ssm_s = run_group(x_sample, p_sample, cache_k, cache_v, state_conv, state_ssm,
                                             cache_k.shape[2])
    return (y_p, y_s, k_p, v_p, conv_p, ssm_p, k_s, v_s, conv_s, ssm_s)
```

```python
import functools
import math

import numpy as np
import jax
import jax.numpy as jnp
from jax import lax
from jax.experimental import pallas as pl
from jax.experimental.pallas import tpu as pltpu

F32 = jnp.float32
BF16 = jnp.bfloat16

EPS = 1e-6
CHUNK = 64
NEG_INF = -1e30
P_TOPK = 16

LANES = 128
SUBLANES = 8
VMEM_LIMIT = 52 * 1024 * 1024

ATT_TQ = 1024
ATT_TK = 1024
ATT_RQ = 512
ROW_TILE = 1024
SSD_CHUNK = 256
PEER_TT = 512

NT_DIMS = (((1,), (1,)), ((), ()))
TN_DIMS = (((0,), (0,)), ((), ()))


def _cparams(*sem, flags=None):
    return pltpu.CompilerParams(dimension_semantics=sem, vmem_limit_bytes=VMEM_LIMIT, flags=flags)


def _rms(x):
    return x * lax.rsqrt(jnp.mean(x * x, axis=-1, keepdims=True) + EPS)


def _split3(x):
    hi = x.astype(BF16)
    r1 = x - hi.astype(F32)
    mid = r1.astype(BF16)
    lo = (r1 - mid.astype(F32)).astype(BF16)
    return hi, mid, lo


def _rms_cast_kernel(x_ref, g_ref, o_ref):
    o_ref[...] = (_rms(x_ref[...]) * g_ref[...]).astype(o_ref.dtype)


def rms_cast(x, g, tm=ROW_TILE):
    n, d = x.shape
    tm = min(tm, n)
    assert n % tm == 0
    return pl.pallas_call(
        _rms_cast_kernel,
        grid=(n // tm,),
        in_specs=[pl.BlockSpec((tm, d), lambda i: (i, 0)), pl.BlockSpec((1, d), lambda i: (0, 0))],
        out_specs=pl.BlockSpec((tm, d), lambda i: (i, 0)),
        out_shape=jax.ShapeDtypeStruct((n, d), BF16),
        compiler_params=_cparams("parallel"),
        name="rms_cast",
    )(x, g.reshape(1, d))


def _matmul_kernel(h_ref, w_ref, *o_refs):
    y = jnp.dot(h_ref[...], w_ref[...], preferred_element_type=F32)
    for o_ref in o_refs:
        o_ref[...] = y.astype(o_ref.dtype)


def matmul(h, w, out_dtypes, tm=ROW_TILE):
    n, d = h.shape
    c = w.shape[1]
    tm = min(tm, n)
    tn = min(c, 1024)
    assert n % tm == 0 and c % tn == 0
    outs = pl.pallas_call(
        _matmul_kernel,
        grid=(c // tn, n // tm),
        in_specs=[pl.BlockSpec((tm, d), lambda j, i: (i, 0)),
                  pl.BlockSpec((d, tn), lambda j, i: (0, j))],
        out_specs=[pl.BlockSpec((tm, tn), lambda j, i: (i, j)) for _ in out_dtypes],
        out_shape=[jax.ShapeDtypeStruct((n, c), dt) for dt in out_dtypes],
        compiler_params=_cparams("parallel", "parallel"),
        name="proj_matmul",
    )(h, w)
    return outs


def _attn_kernel(qi_tab, ki_tab, par_ref, q_ref, k_ref, v_ref, g_ref, o_ref,
                 m_scr, l_scr, acc_scr, *, tq, tk, rq, q_off, lk, n_heads, out_scale):
    h = pl.program_id(1)
    step = pl.program_id(2)
    qi = qi_tab[step]
    ki = ki_tab[step]
    slope = par_ref[h]
    lam = par_ref[n_heads]
    dk = LANES // 2

    qs = q_off + qi * tq
    ks = ki * tk
    last_vis = (((qs + tq - 1) // CHUNK) * CHUNK + CHUNK - 1) // tk
    k_last = jnp.minimum((lk - 1) // tk, last_vis)

    @pl.when(ki == 0)
    def _():
        m_scr[...] = jnp.full(m_scr.shape, NEG_INF, F32)
        l_scr[...] = jnp.zeros(l_scr.shape, F32)
        acc_scr[...] = jnp.zeros(acc_scr.shape, F32)

    k = k_ref[0]
    v = v_ref[0]
    lane = lax.broadcasted_iota(jnp.int32, (rq, LANES), 1)

    def process(r, bias_fn):
        rows = slice(r * rq, (r + 1) * rq)
        q = q_ref[0, rows, :]
        zero = jnp.zeros_like(q)
        qq = jnp.concatenate([jnp.where(lane < dk, q, zero), jnp.where(lane >= dk, q, zero)], axis=0)
        s = lax.dot_general(qq, k, NT_DIMS, preferred_element_type=F32)
        bm = bias_fn(r)
        ps, alphas = [], []
        for j in range(2):
            sj = s[j * rq:(j + 1) * rq] + bm
            m_old = m_scr[j, rows, :]
            m_new = jnp.maximum(m_old, jnp.max(sj, axis=-1, keepdims=True))
            alpha = jnp.exp(m_old - m_new)
            p = jnp.exp(sj - jnp.tile(m_new, (1, tk // LANES)))
            l_scr[j, rows, :] = alpha * l_scr[j, rows, :] + jnp.sum(p, axis=-1, keepdims=True)
            m_scr[j, rows, :] = m_new
            alphas.append(alpha)
            ps.append(p.astype(BF16))
        pv = jnp.dot(jnp.concatenate(ps, axis=0), v, preferred_element_type=F32)
        for j in range(2):
            acc_scr[j, rows, :] = alphas[j] * acc_scr[j, rows, :] + pv[j * rq:(j + 1) * rq]

    def bias_row(r):
        kpos = ks + lax.broadcasted_iota(jnp.int32, (1, tk), 1)
        return slope * (kpos - qs).astype(F32)

    def bias_full(r):
        ii = r * rq + lax.broadcasted_iota(jnp.int32, (rq, tk), 0)
        qpos = qs + ii
        kpos = ks + lax.broadcasted_iota(jnp.int32, (rq, tk), 1)
        dist = jnp.abs(qpos - kpos)
        visible = (lax.shift_right_logical(kpos, 6) <= lax.shift_right_logical(qpos, 6)) & (kpos < lk)
        return jnp.where(visible, slope * (ii - dist).astype(F32), NEG_INF)

    n_sub = tq // rq
    simple = (ks + tk - 1 <= qs) & (ks + tk <= lk)

    @pl.when(simple)
    def _():
        for r in range(n_sub):
            process(r, bias_row)

    @pl.when(jnp.logical_not(simple) & (ki <= k_last))
    def _():
        for r in range(n_sub):
            process(r, bias_full)

    @pl.when(ki == k_last)
    def _():
        o = acc_scr[0] / l_scr[0] - lam * (acc_scr[1] / l_scr[1])
        o = _rms(o) * g_ref[...] * out_scale
        o_ref[0] = o.astype(o_ref.dtype)


def diff_attention(q, k, v, par, sub_g, *, lq, lk, q_off, tq, tk, rq, n_heads, out_scale):
    bsz = q.shape[0]
    lk_pad = k.shape[1]
    assert lq % tq == 0 and lk_pad % tk == 0 and tq % rq == 0 and CHUNK == 64
    nq, nk = lq // tq, lk_pad // tk
    pairs = []
    for qi in range(nq):
        qs = q_off + qi * tq
        k_last = min((lk - 1) // tk, (((qs + tq - 1) // CHUNK) * CHUNK + CHUNK - 1) // tk)
        pairs += [(qi, ki) for ki in range(k_last + 1)]
    qi_tab = jnp.asarray(np.array([p[0] for p in pairs], np.int32))
    ki_tab = jnp.asarray(np.array([p[1] for p in pairs], np.int32))
    kern = functools.partial(_attn_kernel, tq=tq, tk=tk, rq=rq, q_off=q_off, lk=lk,
                             n_heads=n_heads, out_scale=out_scale)
    grid_spec = pltpu.PrefetchScalarGridSpec(
        num_scalar_prefetch=2,
        grid=(bsz, n_heads, len(pairs)),
        in_specs=[pl.BlockSpec(memory_space=pltpu.SMEM),
                  pl.BlockSpec((1, tq, LANES), lambda b, h, s, qt, kt: (b, qt[s], h)),
                  pl.BlockSpec((1, tk, LANES), lambda b, h, s, qt, kt: (b, kt[s], h)),
                  pl.BlockSpec((1, tk, LANES), lambda b, h, s, qt, kt: (b, kt[s], h)),
                  pl.BlockSpec((1, LANES), lambda b, h, s, qt, kt: (0, 0))],
        out_specs=pl.BlockSpec((1, tq, LANES), lambda b, h, s, qt, kt: (b, qt[s], h)),
        scratch_shapes=[pltpu.VMEM((2, tq, LANES), F32), pltpu.VMEM((2, tq, LANES), F32),
                        pltpu.VMEM((2, tq, LANES), F32)])
    return pl.pallas_call(
        kern, grid_spec=grid_spec,
        out_shape=jax.ShapeDtypeStruct((bsz, lq, n_heads * LANES), BF16),
        compiler_params=_cparams("parallel", "parallel", "arbitrary"),
        name="diff_attention",
    )(qi_tab, ki_tab, par, q, k, v, sub_g.reshape(1, LANES))


def _ssm_kernel(z_ref, x_ref, b_ref, c_ref, dt_ref, dtb_ref, alog_ref, dsk_ref,
                cwx_ref, cwb_ref, cwc_ref, cbx_ref, cbb_ref, cbc_ref,
                px_ref, pb_ref, pc_ref, h0_ref, ng_ref,
                yg_ref, hfin_ref, ht_scr, xpx, xpb, xpc, *, lc, l_valid, n_conv):
    c = pl.program_id(2)
    nc = pl.num_programs(2)
    npair = ht_scr.shape[0]
    pad = SUBLANES
    nh = n_conv - 1

    @pl.when(c == 0)
    def _():
        for pr in range(npair):
            ht_scr[pr] = h0_ref[0, pr].T
        xpx[pad - nh:pad, :] = px_ref[0]
        xpb[pad - nh:pad, :] = pb_ref[0]
        xpc[pad - nh:pad, :] = pc_ref[0]

    @pl.when(c > 0)
    def _():
        xpx[pad - nh:pad, :] = xpx[pad + lc - nh:pad + lc, :]
        xpb[pad - nh:pad, :] = xpb[pad + lc - nh:pad + lc, :]
        xpc[pad - nh:pad, :] = xpc[pad + lc - nh:pad + lc, :]

    xpx[pad:pad + lc, :] = x_ref[0]
    xpb[pad:pad + lc, :] = b_ref[0]
    xpc[pad:pad + lc, :] = c_ref[0]

    def conv_silu(xp, cw_ref, cb_ref):
        acc = cb_ref[...]
        for w in range(n_conv):
            acc = acc + cw_ref[w:w + 1, :] * xp[pad - nh + w:pad - nh + w + lc, :]
        return acc * jax.nn.sigmoid(acc)

    xs = conv_silu(xpx, cwx_ref, cbx_ref)
    bm = conv_silu(xpb, cwb_ref, cbb_ref).astype(BF16)
    cm = conv_silu(xpc, cwc_ref, cbc_ref).astype(BF16)

    t_col = c * lc + lax.broadcasted_iota(jnp.int32, (lc, LANES), 0)
    dt_c = jax.nn.softplus(dt_ref[0] + dtb_ref[0])
    dt_c = jnp.where(t_col < l_valid, dt_c, 0.0)
    da_c = dt_c * (-jnp.exp(alog_ref[0]))
    ri = lax.broadcasted_iota(jnp.int32, (lc, lc), 0)
    ci = lax.broadcasted_iota(jnp.int32, (lc, lc), 1)
    causal = ri >= ci
    tril = jnp.where(causal, 1.0, 0.0).astype(BF16)
    acum_c = None
    for piece in _split3(da_c):
        t = jnp.dot(tril, piece, preferred_element_type=F32)
        acum_c = t if acum_c is None else acum_c + t
    acum_r = acum_c.T
    dt_r = dt_c.T
    a_last = acum_c[lc - 1:lc, :]
    wdec_c = jnp.exp(a_last - acum_c) * dt_c
    eac_c = jnp.exp(acum_c)
    ea_last = jnp.exp(a_last)

    cb = lax.dot_general(cm, bm, NT_DIMS, preferred_element_type=F32)
    lane = lax.broadcasted_iota(jnp.int32, (lc, LANES), 1)
    lane1 = lax.broadcasted_iota(jnp.int32, (1, LANES), 1)
    half = LANES // 2
    ys = []
    for pr in range(npair):
        xp_f = xs[:, pr * LANES:(pr + 1) * LANES]
        y = dsk_ref[:, pr * LANES:(pr + 1) * LANES] * xp_f
        for e in range(2):
            hh = 2 * pr + e
            seg = acum_c[:, hh:hh + 1] - acum_r[hh:hh + 1, :]
            lmat = jnp.exp(jnp.where(causal, seg, NEG_INF))
            m = (cb * lmat * dt_r[hh:hh + 1, :]).astype(BF16)
            sel = (lane < half) if e == 0 else (lane >= half)
            xh = jnp.where(sel, xp_f, 0.0).astype(BF16)
            y = y + jnp.dot(m, xh, preferred_element_type=F32)
        h_a, h_b = 2 * pr, 2 * pr + 1
        scale = jnp.where(lane < half, eac_c[:, h_a:h_a + 1], eac_c[:, h_b:h_b + 1])
        ht = ht_scr[pr]
        y = y + jnp.dot(cm, ht.astype(BF16), preferred_element_type=F32) * scale
        wd = jnp.where(lane < half, wdec_c[:, h_a:h_a + 1], wdec_c[:, h_b:h_b + 1])
        dec = jnp.where(lane1 < half, ea_last[:, h_a:h_a + 1], ea_last[:, h_b:h_b + 1])
        upd = lax.dot_general(bm, (wd * xp_f).astype(BF16), TN_DIMS, preferred_element_type=F32)
        ht_scr[pr] = dec * ht + upd
        ys.append(y)
    y = jnp.concatenate(ys, axis=-1)

    zz = z_ref[0]
    yg = y * (zz * jax.nn.sigmoid(zz))
    yg_ref[0] = (_rms(yg) * ng_ref[...]).astype(yg_ref.dtype)

    @pl.when(c == nc - 1)
    def _():
        for pr in range(npair):
            hfin_ref[0, pr] = ht_scr[pr].T


def ssm_branch(z, xbc, dt_pad, conv_past, h0, conv_w, conv_b, dtb_pad, alog_pad, dsk_exp, norm_g,
               *, lc, l_valid, n_groups, d_inner, d_state):
    bsz, length, _ = z.shape
    assert length % lc == 0 and lc % LANES == 0
    nc = length // lc
    gw = d_inner // n_groups
    assert gw % LANES == 0 and d_state == LANES
    npair = gw // LANES
    n_conv = conv_w.shape[0]
    xb0 = d_inner // d_state
    cb0 = xb0 + n_groups
    kern = functools.partial(_ssm_kernel, lc=lc, l_valid=l_valid, n_conv=n_conv)
    seq = lambda width, off: pl.BlockSpec((1, lc, width), lambda b, g, c: (b, c, off + g))
    par = lambda rows, width, off: pl.BlockSpec((rows, width), lambda b, g, c: (0, off + g))
    hist = lambda width, off: pl.BlockSpec((1, n_conv - 1, width), lambda b, g, c: (b, 0, off + g))
    state_spec = pl.BlockSpec((1, npair, LANES, d_state), lambda b, g, c: (b, g, 0, 0))
    yg, hfin = pl.pallas_call(
        kern,
        grid=(bsz, n_groups, nc),
        in_specs=[seq(gw, 0), seq(gw, 0), seq(d_state, xb0), seq(d_state, cb0), seq(LANES, 0),
                  pl.BlockSpec((1, 1, LANES), lambda b, g, c: (g, 0, 0)),
                  pl.BlockSpec((1, 1, LANES), lambda b, g, c: (g, 0, 0)),
                  par(1, gw, 0),
                  par(n_conv, gw, 0), par(n_conv, d_state, xb0), par(n_conv, d_state, cb0),
                  par(1, gw, 0), par(1, d_state, xb0), par(1, d_state, cb0),
                  hist(gw, 0), hist(d_state, xb0), hist(d_state, cb0),
                  state_spec, par(1, gw, 0)],
        out_specs=[seq(gw, 0), state_spec],
        out_shape=[jax.ShapeDtypeStruct((bsz, length, d_inner), BF16),
                   jax.ShapeDtypeStruct(h0.shape, F32)],
        scratch_shapes=[pltpu.VMEM((npair, d_state, LANES), F32),
                        pltpu.VMEM((lc + 2 * SUBLANES, gw), F32),
                        pltpu.VMEM((lc + 2 * SUBLANES, d_state), F32),
                        pltpu.VMEM((lc + 2 * SUBLANES, d_state), F32)],
        compiler_params=_cparams("parallel", "parallel", "arbitrary"),
        name="ssm_branch",
    )(z, xbc, xbc, xbc, dt_pad, dtb_pad, alog_pad, dsk_exp,
      conv_w, conv_w, conv_w, conv_b, conv_b, conv_b,
      conv_past, conv_past, conv_past, h0, norm_g)
    return yg, hfin


def _mix_kernel(x_ref, a_ref, y_ref, ga_ref, gs_ref, wa_ref, ws_ref, wo_ref, o_ref):
    o_a = jnp.dot(a_ref[...], wa_ref[...], preferred_element_type=F32)
    o_s = jnp.dot(y_ref[...], ws_ref[...], preferred_element_type=F32)
    mix = jax.nn.sigmoid(ga_ref[...]) * o_a + jax.nn.sigmoid(gs_ref[...]) * o_s
    o_ref[...] = x_ref[...] + jnp.dot(mix.astype(BF16), wo_ref[...], preferred_element_type=F32)


def mixer_out(x, attn, yg, gates, wa, ws, wo, tm=512):
    n, d = x.shape
    tm = min(tm, n)
    assert n % tm == 0 and gates.shape[1] == 2 * d
    row = lambda width, j=0: pl.BlockSpec((tm, width), lambda i, j=j: (i, j))
    full = lambda a: pl.BlockSpec(a.shape, lambda i: (0, 0))
    return pl.pallas_call(
        _mix_kernel,
        grid=(n // tm,),
        in_specs=[row(d), row(attn.shape[1]), row(yg.shape[1]), row(d, 0), row(d, 1),
                  full(wa), full(ws), full(wo)],
        out_specs=row(d),
        out_shape=jax.ShapeDtypeStruct((n, d), F32),
        compiler_params=_cparams("parallel"),
        name="mixer_out",
    )(x, attn, yg, gates, gates, wa, ws, wo)


def _staircase_pairs(k):
    return [(i, j) for i in range(k) for j in range(k) if (i + 1) * (j + 1) <= k]


def _oddeven_merge(lo, hi, r):
    step = r * 2
    if step < hi - lo:
        yield from _oddeven_merge(lo, hi, step)
        yield from _oddeven_merge(lo + r, hi, step)
        yield from [(i, i + r) for i in range(lo + r, hi - r, step)]
    else:
        yield (lo, lo + r)


def _oddeven_sort_pairs(lo, hi):
    if hi - lo >= 1:
        mid = lo + (hi - lo) // 2
        yield from _oddeven_sort_pairs(lo, mid)
        yield from _oddeven_sort_pairs(mid + 1, hi)
        yield from _oddeven_merge(lo, hi, 1)


def _sort_desc(vals):
    vals = list(vals)
    for i, j in _oddeven_sort_pairs(0, len(vals) - 1):
        vals[i], vals[j] = jnp.maximum(vals[i], vals[j]), jnp.minimum(vals[i], vals[j])
    return vals


def _bitonic_merge_desc(vals):
    vals = list(vals)
    n = len(vals)
    d = n // 2
    while d >= 1:
        for i in range(n):
            if i & d == 0:
                vals[i], vals[i + d] = jnp.maximum(vals[i], vals[i + d]), jnp.minimum(vals[i], vals[i + d])
        d //= 2
    return vals


def _merge_sublanes_top(vals):
    n = len(vals)
    shift = SUBLANES // 2
    while shift >= 1:
        other = [pltpu.roll(v, shift, 0) for v in vals]
        vals = _bitonic_merge_desc([jnp.maximum(vals[i], other[n - 1 - i]) for i in range(n)])
        shift //= 2
    return vals


def _router_kernel(x_ref, g_ref, wqt_ref, k1_ref, k2_ref,
                   h_ref, n1_ref, e1_ref, rk2_ref, e2_ref, qt_scr, *, n_heads, pairs):
    tt = x_ref.shape[0]
    nk = k1_ref.shape[0]
    ht = (_rms(x_ref[...]) * g_ref[...]).T.astype(BF16)
    h_ref[...] = ht
    qt_scr[...] = jnp.dot(wqt_ref[...], ht, preferred_element_type=F32).astype(BF16)
    sub = lax.broadcasted_iota(jnp.int32, (SUBLANES, LANES), 0)
    n_cv = -(-len(pairs) // SUBLANES)
    n_cv_pad = pl.next_power_of_2(n_cv)

    n_rows = nk // SUBLANES

    def head_tile(s1, s2, ls):
        s1cols = [s1[r * SUBLANES:(r + 1) * SUBLANES, ls] for r in range(n_rows)]
        s2cols = [s2[r * SUBLANES:(r + 1) * SUBLANES, ls] for r in range(n_rows)]
        v1 = _merge_sublanes_top(_sort_desc(s1cols))[:P_TOPK]
        v2 = _merge_sublanes_top(_sort_desc(s2cols))[:P_TOPK]
        sums = {(i, j): v1[i] + v2[j] for (i, j) in pairs}
        cvs = []
        for c in range(n_cv_pad):
            cv = jnp.full((SUBLANES, LANES), -jnp.inf, F32)
            for k in range(SUBLANES):
                pidx = c * SUBLANES + k
                if pidx < len(pairs):
                    cv = jnp.where(sub == k, sums[pairs[pidx]], cv)
            cvs.append(cv)
        top = _merge_candidates(_sort_desc(cvs))
        tau = top[P_TOPK - 1]
        zsum = jnp.zeros((SUBLANES, LANES), F32)
        for r in range(P_TOPK):
            zsum = zsum + jnp.exp(top[r] - top[0])
        cnts = []
        for i in range(P_TOPK):
            cnt = jnp.zeros((SUBLANES, LANES), F32)
            for (pi, pj) in pairs:
                if pi == i:
                    cnt = jnp.where(sums[(pi, pj)] >= tau, float(pj + 1), cnt)
            cnts.append(cnt)
        n1, e1, rk2, e2 = [], [], [], []
        for r in range(n_rows):
            nrow = jnp.zeros((SUBLANES, LANES), F32)
            rank = jnp.full((SUBLANES, LANES), float(P_TOPK), F32)
            for i in range(P_TOPK):
                nrow = jnp.where(s1cols[r] == v1[i], cnts[i], nrow)
            for j in reversed(range(P_TOPK)):
                rank = jnp.where(v2[j] <= s2cols[r], float(j), rank)
            n1.append(nrow)
            rk2.append(rank)
            e1.append(jnp.exp(s1cols[r] - v1[0]))
            e2.append(jnp.exp(s2cols[r] - v2[0]) / zsum)
        return n1, e1, rk2, e2

    def pack_pair(lo, hi):
        lo_w = pltpu.bitcast(lo.astype(BF16).astype(F32), jnp.uint32)
        hi_w = pltpu.bitcast(hi.astype(BF16).astype(F32), jnp.uint32)
        return hi_w | lax.shift_right_logical(lo_w, jnp.uint32(16))

    half = n_heads // 2

    def head_pair(hp, carry):
        scores = []
        for hd in (hp, hp + half):
            qh = qt_scr[pl.ds(pl.multiple_of(hd * nk, nk), nk), :]
            scores.append((jnp.dot(k1_ref[...], qh, preferred_element_type=F32),
                           jnp.dot(k2_ref[...], qh, preferred_element_type=F32)))
        for lg in range(tt // LANES):
            ls = slice(lg * LANES, (lg + 1) * LANES)
            lo = head_tile(scores[0][0], scores[0][1], ls)
            hi = head_tile(scores[1][0], scores[1][1], ls)
            for out_ref, lo_l, hi_l in zip((n1_ref, e1_ref, rk2_ref, e2_ref), lo, hi):
                for r in range(n_rows):
                    out_ref[0, hp, r * SUBLANES:(r + 1) * SUBLANES, ls] = pack_pair(lo_l[r], hi_l[r])
        return carry

    lax.fori_loop(0, half, head_pair, 0)


def _merge_candidates(cvs):
    assert len(cvs) == SUBLANES and P_TOPK == 2 * SUBLANES
    other = [pltpu.roll(v, SUBLANES // 2, 0) for v in cvs]
    vals = _bitonic_merge_desc(cvs + other[::-1])
    for shift in (SUBLANES // 4, SUBLANES // 8):
        oth = [pltpu.roll(v, shift, 0) for v in vals]
        vals = _bitonic_merge_desc([jnp.maximum(vals[i], oth[P_TOPK - 1 - i]) for i in range(P_TOPK)])
    return vals


def peer_router(x, g, wqt, k1p, k2p, *, n_heads, pairs, tt):
    n, d = x.shape
    assert n % tt == 0 and tt % LANES == 0
    nt = n // tt
    nk = k1p.shape[0]
    kern = functools.partial(_router_kernel, n_heads=n_heads, pairs=pairs)
    full = lambda a: pl.BlockSpec(a.shape, lambda i: (0,) * a.ndim)
    assert n_heads % 2 == 0
    tspec = pl.BlockSpec((1, n_heads // 2, nk, tt), lambda i: (i, 0, 0, 0))
    tshape = jax.ShapeDtypeStruct((nt, n_heads // 2, nk, tt), jnp.uint32)
    return pl.pallas_call(
        kern,
        grid=(nt,),
        in_specs=[pl.BlockSpec((tt, d), lambda i: (i, 0)), full(g), full(wqt), full(k1p), full(k2p)],
        out_specs=[pl.BlockSpec((d, tt), lambda i: (0, i)), tspec, tspec, tspec, tspec],
        out_shape=[jax.ShapeDtypeStruct((d, n), BF16), tshape, tshape, tshape, tshape],
        scratch_shapes=[pltpu.VMEM((wqt.shape[0], tt), BF16)],
        compiler_params=_cparams("parallel"),
        name="peer_router",
    )(x, g, wqt, k1p, k2p)


def _gelu_tanh(x):
    c = math.sqrt(2.0 / math.pi)
    return (0.5 * x) * (1.0 + jnp.tanh(x * (c + (c * 0.044715) * (x * x))))


def _experts_kernel(x_ref, h_ref, n1r_ref, e1r_ref, rk2_ref, e2_ref, u_ref, vt_ref,
                    o_ref, acc_scr, a_scr, *, rows_per_step, bh):
    s = pl.program_id(1)
    n_pairs, nk = rk2_ref.shape[1], rk2_ref.shape[2]
    tt = h_ref.shape[1]
    as_bf16 = lambda w: pltpu.bitcast(w, BF16)

    @pl.when(s == 0)
    def _():
        acc_scr[...] = jnp.zeros(acc_scr.shape, F32)

    pt = jnp.dot(u_ref[...], h_ref[...], preferred_element_type=F32)
    for a in range(rows_per_step):
        for lg in range(tt // LANES):
            ls = slice(lg * LANES, (lg + 1) * LANES)
            for b0 in range(0, nk, bh):
                gsum = jnp.zeros((2 * bh, LANES), BF16)
                for hp in range(n_pairs):
                    n1 = as_bf16(jnp.broadcast_to(n1r_ref[0, hp, a:a + 1, ls], (bh, LANES)))
                    e1 = as_bf16(jnp.broadcast_to(e1r_ref[0, hp, a:a + 1, ls], (bh, LANES)))
                    keep = as_bf16(rk2_ref[0, hp, b0:b0 + bh, ls]) < n1
                    w = e1 * as_bf16(e2_ref[0, hp, b0:b0 + bh, ls])
                    gsum = gsum + jnp.where(keep, w, jnp.zeros_like(w))
                gw = pltpu.bitcast(gsum, jnp.uint32)
                gate = (pltpu.bitcast(lax.shift_left(gw, jnp.uint32(16)), F32)
                        + pltpu.bitcast(gw & jnp.uint32(0xFFFF0000), F32))
                r0 = a * nk + b0
                a_scr[r0:r0 + bh, ls] = (_gelu_tanh(pt[r0:r0 + bh, ls]) * gate).astype(BF16)
    acc_scr[...] += jnp.dot(vt_ref[...], a_scr[...], preferred_element_type=F32)

    @pl.when(s == pl.num_programs(1) - 1)
    def _():
        o_ref[...] = x_ref[...] + acc_scr[...].T


def peer_experts(x, h, n1, e1, rk2, e2, u, vt):
    n, d = x.shape
    nt, n_pairs, nk, tt = rk2.shape
    n_exp = u.shape[0]
    rows_per_step = SUBLANES
    bh = 8 * SUBLANES
    eb = rows_per_step * nk
    assert n_exp % eb == 0 and n == nt * tt and nk % bh == 0
    kern = functools.partial(_experts_kernel, rows_per_step=rows_per_step, bh=bh)
    tspec = pl.BlockSpec((1, n_pairs, nk, tt), lambda i, s: (i, 0, 0, 0))
    rspec = pl.BlockSpec((1, n_pairs, rows_per_step, tt), lambda i, s: (i, 0, s, 0))
    return pl.pallas_call(
        kern,
        grid=(nt, n_exp // eb),
        in_specs=[pl.BlockSpec((tt, d), lambda i, s: (i, 0)),
                  pl.BlockSpec((d, tt), lambda i, s: (0, i)),
                  rspec, rspec, tspec, tspec,
                  pl.BlockSpec((eb, d), lambda i, s: (s, 0)),
                  pl.BlockSpec((d, eb), lambda i, s: (0, s))],
        out_specs=pl.BlockSpec((tt, d), lambda i, s: (i, 0)),
        out_shape=jax.ShapeDtypeStruct((n, d), F32),
        scratch_shapes=[pltpu.VMEM((d, tt), F32), pltpu.VMEM((eb, tt), BF16)],
        compiler_params=_cparams("parallel", "arbitrary"),
        name="peer_experts",
    )(x, h, n1, e1, rk2, e2, u, vt)


def _ple_kernel(x_ref, p_ref, g_ref, wg_ref, wp_ref, gf_ref, o_ref, *, final):
    x = x_ref[...]
    hg = (_rms(x) * g_ref[...]).astype(BF16)
    gate = jax.nn.sigmoid(jnp.dot(hg, wg_ref[...], preferred_element_type=F32))
    pe = jnp.dot(p_ref[...].astype(BF16), wp_ref[...], preferred_element_type=F32)
    x = x + gate * pe
    if final:
        x = _rms(x) * gf_ref[...]
    o_ref[...] = x


def ple_out(x, p, g, wg, wp, gf, *, final, tm=512):
    n, d = x.shape
    tm = min(tm, n)
    assert n % tm == 0
    full = lambda a: pl.BlockSpec(a.shape, lambda i: (0, 0))
    return pl.pallas_call(
        functools.partial(_ple_kernel, final=final),
        grid=(n // tm,),
        in_specs=[pl.BlockSpec((tm, d), lambda i: (i, 0)), pl.BlockSpec((tm, p.shape[1]), lambda i: (i, 0)),
                  full(g), full(wg), full(wp), full(gf)],
        out_specs=pl.BlockSpec((tm, d), lambda i: (i, 0)),
        out_shape=jax.ShapeDtypeStruct((n, d), F32),
        compiler_params=_cparams("parallel"),
        name="ple_out",
    )(x, p, g, wg, wp, gf)


def _round_up(x, m):
    return -(-x // m) * m


def _layer(x, p_i, k_past, v_past, conv_past, ssm_past, q_off, lw, cfg, lam_init, final_g):
    bsz, length, d = x.shape
    n = bsz * length
    n_ah, a_dk, a_dv = cfg["a_heads"], cfg["a_dk"], cfg["a_dv"]
    d_inner, n_groups, d_state = cfg["d_inner"], cfg["s_groups"], cfg["d_state"]
    x2 = x.reshape(n, d)

    hn = rms_cast(x2, lw["norm_mix_g"])
    (q,) = matmul(hn, lw["w_q"], [BF16])
    k_f32, k_bf = matmul(hn, lw["w_k"], [F32, BF16])
    v_f32, v_bf = matmul(hn, lw["w_v"], [F32, BF16])
    (z,) = matmul(hn, lw["w_z"], [F32])
    (xbc,) = matmul(hn, lw["w_xbc"], [F32])
    (dt_pad,) = matmul(hn, lw["w_dt"], [F32])
    (gates,) = matmul(hn, lw["w_gate"], [F32])

    hw = n_ah * LANES
    q3 = q.reshape(bsz, length, hw)
    k3 = k_bf.reshape(bsz, length, hw)
    v3 = v_bf.reshape(bsz, length, hw)
    if k_past is None:
        lk = length
        tq, tk = min(ATT_TQ, length), min(ATT_TK, length)
        rq = min(ATT_RQ, tq)
    else:
        past = k_past.shape[1]
        lk = past + length
        lk_pad = _round_up(lk, LANES)
        zpad = jnp.zeros((bsz, lk_pad - lk, hw), BF16)
        k3 = jnp.concatenate([k_past.reshape(bsz, past, hw).astype(BF16), k3, zpad], axis=1)
        v3 = jnp.concatenate([v_past.reshape(bsz, past, hw).astype(BF16), v3, zpad], axis=1)
        tq, tk, rq = length, lk_pad, length
    attn = diff_attention(q3, k3, v3, lw["attn_par"], lw["attn_subln_g"], lq=length, lk=lk, q_off=q_off,
                          tq=tq, tk=tk, rq=rq, n_heads=n_ah, out_scale=1.0 - lam_init)

    lc = min(SSD_CHUNK, _round_up(length, LANES))
    l_pad = _round_up(length, lc)
    z3 = z.reshape(bsz, length, d_inner)
    xbc3 = xbc.reshape(bsz, length, -1)
    dt3 = dt_pad.reshape(bsz, length, -1)
    if l_pad != length:
        padt = lambda a: jnp.pad(a, ((0, 0), (0, l_pad - length), (0, 0)))
        z3p, xbc3p, dt3p = padt(z3), padt(xbc3), padt(dt3)
    else:
        z3p, xbc3p, dt3p = z3, xbc3, dt3
    n_conv = lw["conv_w"].shape[0]
    if conv_past is None:
        conv_past = jnp.zeros((bsz, n_conv - 1, xbc3.shape[-1]), F32)
    if ssm_past is None:
        ssm_past = jnp.zeros((bsz, cfg["s_heads"], cfg["s_head_p"], d_state), F32)
    h0 = ssm_past.astype(F32).reshape(bsz, -1, LANES, d_state)
    yg, hfin = ssm_branch(z3p, xbc3p, dt3p, conv_past.astype(F32), h0, lw["conv_w"], lw["conv_b"],
                          lw["dtb_pad"], lw["alog_pad"], lw["dsk_exp"], lw["ssm_norm_g"],
                          lc=lc, l_valid=length, n_groups=n_groups, d_inner=d_inner, d_state=d_state)
    yg = yg[:, :length].reshape(n, d_inner)
    ssm_new = hfin.reshape(ssm_past.shape)
    if length >= n_conv - 1:
        conv_new = xbc3[:, length - (n_conv - 1):]
    else:
        conv_new = jnp.concatenate([conv_past.astype(F32), xbc3], axis=1)[:, -(n_conv - 1):]

    x2 = mixer_out(x2, attn.reshape(n, hw), yg, gates, lw["w_attn_o"], lw["w_ssm_o"], lw["w_out"])

    tt = min(PEER_TT, n)
    hp, n1, e1, rk2, e2 = peer_router(
        x2, lw["norm_ffn_g"], lw["peer_wqt"], lw["k1p"], lw["k2p"],
        n_heads=cfg["p_heads"], pairs=cfg["pairs"], tt=tt)
    x2 = peer_experts(x2, hp, n1, e1, rk2, e2, lw["peer_u"], lw["peer_vt"])

    x2 = ple_out(x2, p_i.reshape(n, -1), lw["norm_ple_g"], lw["ple_w_gate"], lw["ple_w_proj"],
                 lw["norm_ple_g"] if final_g is None else final_g, final=final_g is not None)

    k_new = k_f32.reshape(bsz, length, n_ah, 2, a_dk)
    v_new = v_f32.reshape(bsz, length, n_ah, a_dv)
    return x2.reshape(bsz, length, d), k_new, v_new, conv_new, ssm_new


def kernel(x_prompt, x_sample, p_prompt, p_sample, cache_k, cache_v, state_conv, state_ssm, norm_mix_g, w_in, lam_q1, lam_k1, lam_q2, lam_k2, attn_subln_g, w_attn_o, conv_w, conv_b, dt_bias, a_log, d_skip, ssm_norm_g, w_ssm_o, w_out, norm_ffn_g, peer_wq, peer_k1, peer_k2, peer_u, peer_v, norm_ple_g, ple_w_gate, ple_w_proj, final_norm_g):
    depth = w_in.shape[0]
    d = x_prompt.shape[-1]
    a_heads, a_dk, a_dv = cache_k.shape[3], cache_k.shape[5], cache_v.shape[4]
    s_heads, s_head_p, d_state = state_ssm.shape[2:]
    d_inner = ssm_norm_g.shape[1]
    conv_dim = conv_w.shape[2]
    s_groups = (conv_dim - d_inner) // (2 * d_state)
    hpg = s_heads // s_groups
    p_nkeys, p_half = peer_k1.shape[1], peer_k1.shape[2]
    p_heads = peer_wq.shape[2] // (2 * p_half)
    assert 2 * a_dk == LANES and a_dv == LANES and 2 * s_head_p == LANES and d_state == LANES
    assert p_nkeys == LANES and 2 * p_half == LANES and hpg <= LANES
    pairs = _staircase_pairs(P_TOPK)
    assert len(pairs) <= SUBLANES * SUBLANES
    cfg = dict(a_heads=a_heads, a_dk=a_dk, a_dv=a_dv, d_inner=d_inner, s_groups=s_groups, d_state=d_state,
               s_heads=s_heads, s_head_p=s_head_p, p_heads=p_heads, pairs=pairs)

    c_q = a_heads * 2 * a_dk
    c_v = a_heads * a_dv
    splits = np.cumsum([0, c_q, c_q, c_v, d_inner, conv_dim, s_heads, 2 * d])
    row = lambda a: a.astype(F32).reshape(1, -1)

    layers = []
    for i in range(depth):
        w = w_in[i]
        sl = lambda j: w[:, splits[j]:splits[j + 1]]
        w_dt = jnp.zeros((d, s_groups, LANES), F32).at[:, :, :hpg].set(sl(5).reshape(d, s_groups, hpg))
        grp = lambda a, fill: jnp.full((s_groups, 1, LANES), fill, F32).at[:, 0, :hpg].set(
            a.astype(F32).reshape(s_groups, hpg))
        f32 = jnp.float32
        lam_init = 0.8 - 0.6 * math.exp(-0.3 * i)
        lam = (jnp.exp(jnp.sum(lam_q1[i].astype(f32) * lam_k1[i].astype(f32)))
               - jnp.exp(jnp.sum(lam_q2[i].astype(f32) * lam_k2[i].astype(f32))) + lam_init)
        slopes = 2.0 ** (-8.0 * (jnp.arange(a_heads, dtype=f32) + 1.0) / a_heads)
        zeros_half = jnp.zeros((p_nkeys, p_half), F32)
        lw = dict(
            norm_mix_g=norm_mix_g[i],
            w_q=(sl(0) * (a_dk ** -0.5)).astype(BF16), w_k=sl(1).astype(BF16), w_v=sl(2).astype(BF16),
            w_z=sl(3).astype(BF16), w_xbc=sl(4).astype(BF16), w_gate=sl(6).astype(BF16),
            w_dt=w_dt.reshape(d, s_groups * LANES).astype(BF16),
            attn_par=jnp.concatenate([slopes, lam.reshape(1)]).astype(F32),
            attn_subln_g=attn_subln_g[i].astype(F32),
            w_attn_o=w_attn_o[i].astype(BF16), w_ssm_o=w_ssm_o[i].astype(BF16), w_out=w_out[i].astype(BF16),
            conv_w=conv_w[i].astype(F32), conv_b=row(conv_b[i]),
            dtb_pad=grp(dt_bias[i], 0.0), alog_pad=grp(a_log[i], 0.0),
            dsk_exp=row(jnp.repeat(d_skip[i], s_head_p)), ssm_norm_g=row(ssm_norm_g[i]),
            norm_ffn_g=row(norm_ffn_g[i]), peer_wqt=peer_wq[i].astype(BF16).T,
            k1p=jnp.concatenate([peer_k1[i], zeros_half], axis=1).astype(BF16),
            k2p=jnp.concatenate([zeros_half, peer_k2[i]], axis=1).astype(BF16),
            peer_u=peer_u[i].astype(BF16), peer_vt=peer_v[i].astype(BF16).T,
            norm_ple_g=row(norm_ple_g[i]), ple_w_gate=ple_w_gate[i].astype(BF16),
            ple_w_proj=ple_w_proj[i].astype(BF16),
        )
        layers.append((lw, lam_init))

    def run_group(x, p, ck, cv, cconv, cssm, q_off):
        k_rows, v_rows, conv_rows, ssm_rows = [], [], [], []
        for i, (lw, lam_init) in enumerate(layers):
            last = i == depth - 1
            x, k_new, v_new, conv_new, ssm_new = _layer(
                x, p[i],
                None if ck is None else ck[i], None if cv is None else cv[i],
                None if cconv is None else cconv[i], None if cssm is None else cssm[i],
                q_off, lw, cfg, lam_init, row(final_norm_g) if last else None)
            k_rows.append(k_new)
            v_rows.append(v_new)
            conv_rows.append(conv_new)
            ssm_rows.append(ssm_new)
        return x, jnp.stack(k_rows), jnp.stack(v_rows), jnp.stack(conv_rows), jnp.stack(ssm_rows)

    y_p, k_p, v_p, conv_p, ssm_p = run_group(x_prompt, p_prompt, None, None, None, None, 0)
    y_s, k_s, v_s, conv_s, ssm_s = run_group(x_sample, p_sample, cache_k, cache_v, state_conv, state_ssm,
                                             cache_k.shape[2])
    return (y_p, y_s, k_p, v_p, conv_p, ssm_p, k_s, v_s, conv_s, ssm_s)
```

```python
import functools
import math

import numpy as np
import jax
import jax.numpy as jnp
from jax import lax
from jax.experimental import pallas as pl
from jax.experimental.pallas import tpu as pltpu

F32 = jnp.float32
BF16 = jnp.bfloat16

EPS = 1e-6
CHUNK = 64
NEG_INF = -1e30
P_TOPK = 16

LANES = 128
SUBLANES = 8
VMEM_LIMIT = 52 * 1024 * 1024

ATT_TQ = 1024
ATT_TK = 1024
ATT_RQ = 512
ROW_TILE = 1024
SSD_CHUNK = 256
PEER_TT = 512

NT_DIMS = (((1,), (1,)), ((), ()))
TN_DIMS = (((0,), (0,)), ((), ()))


def _cparams(*sem, flags=None):
    return pltpu.CompilerParams(dimension_semantics=sem, vmem_limit_bytes=VMEM_LIMIT, flags=flags)


def _rms(x):
    return x * lax.rsqrt(jnp.mean(x * x, axis=-1, keepdims=True) + EPS)


def _split3(x):
    hi = x.astype(BF16)
    r1 = x - hi.astype(F32)
    mid = r1.astype(BF16)
    lo = (r1 - mid.astype(F32)).astype(BF16)
    return hi, mid, lo


def _rms_cast_kernel(x_ref, g_ref, o_ref):
    o_ref[...] = (_rms(x_ref[...]) * g_ref[...]).astype(o_ref.dtype)


def rms_cast(x, g, tm=ROW_TILE):
    n, d = x.shape
    tm = min(tm, n)
    assert n % tm == 0
    return pl.pallas_call(
        _rms_cast_kernel,
        grid=(n // tm,),
        in_specs=[pl.BlockSpec((tm, d), lambda i: (i, 0)), pl.BlockSpec((1, d), lambda i: (0, 0))],
        out_specs=pl.BlockSpec((tm, d), lambda i: (i, 0)),
        out_shape=jax.ShapeDtypeStruct((n, d), BF16),
        compiler_params=_cparams("parallel"),
        name="rms_cast",
    )(x, g.reshape(1, d))


def _matmul_kernel(h_ref, w_ref, *o_refs):
    y = jnp.dot(h_ref[...], w_ref[...], preferred_element_type=F32)
    for o_ref in o_refs:
        o_ref[...] = y.astype(o_ref.dtype)


def matmul(h, w, out_dtypes, tm=ROW_TILE):
    n, d = h.shape
    c = w.shape[1]
    tm = min(tm, n)
    tn = min(c, 1024)
    assert n % tm == 0 and c % tn == 0
    outs = pl.pallas_call(
        _matmul_kernel,
        grid=(c // tn, n // tm),
        in_specs=[pl.BlockSpec((tm, d), lambda j, i: (i, 0)),
                  pl.BlockSpec((d, tn), lambda j, i: (0, j))],
        out_specs=[pl.BlockSpec((tm, tn), lambda j, i: (i, j)) for _ in out_dtypes],
        out_shape=[jax.ShapeDtypeStruct((n, c), dt) for dt in out_dtypes],
        compiler_params=_cparams("parallel", "parallel"),
        name="proj_matmul",
    )(h, w)
    return outs


def _attn_kernel(qi_tab, ki_tab, par_ref, q_ref, k_ref, v_ref, g_ref, o_ref,
                 m_scr, l_scr, acc_scr, *, tq, tk, rq, q_off, lk, n_heads, out_scale):
    h = pl.program_id(1)
    step = pl.program_id(2)
    qi = qi_tab[step]
    ki = ki_tab[step]
    slope = par_ref[h]
    lam = par_ref[n_heads]
    dk = LANES // 2

    qs = q_off + qi * tq
    ks = ki * tk
    last_vis = (((qs + tq - 1) // CHUNK) * CHUNK + CHUNK - 1) // tk
    k_last = jnp.minimum((lk - 1) // tk, last_vis)

    @pl.when(ki == 0)
    def _():
        m_scr[...] = jnp.full(m_scr.shape, NEG_INF, F32)
        l_scr[...] = jnp.zeros(l_scr.shape, F32)
        acc_scr[...] = jnp.zeros(acc_scr.shape, F32)

    k = k_ref[0]
    v = v_ref[0]
    lane = lax.broadcasted_iota(jnp.int32, (rq, LANES), 1)

    def process(r, bias_fn):
        rows = slice(r * rq, (r + 1) * rq)
        q = q_ref[0, rows, :]
        zero = jnp.zeros_like(q)
        qq = jnp.concatenate([jnp.where(lane < dk, q, zero), jnp.where(lane >= dk, q, zero)], axis=0)
        s = lax.dot_general(qq, k, NT_DIMS, preferred_element_type=F32)
        bm = bias_fn(r)
        ps, alphas = [], []
        for j in range(2):
            sj = s[j * rq:(j + 1) * rq] + bm
            m_old = m_scr[j, rows, :]
            m_new = jnp.maximum(m_old, jnp.max(sj, axis=-1, keepdims=True))
            alpha = jnp.exp(m_old - m_new)
            p = jnp.exp(sj - jnp.tile(m_new, (1, tk // LANES)))
            l_scr[j, rows, :] = alpha * l_scr[j, rows, :] + jnp.sum(p, axis=-1, keepdims=True)
            m_scr[j, rows, :] = m_new
            alphas.append(alpha)
            ps.append(p.astype(BF16))
        pv = jnp.dot(jnp.concatenate(ps, axis=0), v, preferred_element_type=F32)
        for j in range(2):
            acc_scr[j, rows, :] = alphas[j] * acc_scr[j, rows, :] + pv[j * rq:(j + 1) * rq]

    def bias_row(r):
        kpos = ks + lax.broadcasted_iota(jnp.int32, (1, tk), 1)
        return slope * (kpos - qs).astype(F32)

    def bias_full(r):
        ii = r * rq + lax.broadcasted_iota(jnp.int32, (rq, tk), 0)
        qpos = qs + ii
        kpos = ks + lax.broadcasted_iota(jnp.int32, (rq, tk), 1)
        dist = jnp.abs(qpos - kpos)
        visible = (lax.shift_right_logical(kpos, 6) <= lax.shift_right_logical(qpos, 6)) & (kpos < lk)
        return jnp.where(visible, slope * (ii - dist).astype(F32), NEG_INF)

    n_sub = tq // rq
    simple = (ks + tk - 1 <= qs) & (ks + tk <= lk)

    @pl.when(simple)
    def _():
        for r in range(n_sub):
            process(r, bias_row)

    @pl.when(jnp.logical_not(simple) & (ki <= k_last))
    def _():
        for r in range(n_sub):
            process(r, bias_full)

    @pl.when(ki == k_last)
    def _():
        o = acc_scr[0] / l_scr[0] - lam * (acc_scr[1] / l_scr[1])
        o = _rms(o) * g_ref[...] * out_scale
        o_ref[0] = o.astype(o_ref.dtype)


def diff_attention(q, k, v, par, sub_g, *, lq, lk, q_off, tq, tk, rq, n_heads, out_scale):
    bsz = q.shape[0]
    lk_pad = k.shape[1]
    assert lq % tq == 0 and lk_pad % tk == 0 and tq % rq == 0 and CHUNK == 64
    nq, nk = lq // tq, lk_pad // tk
    pairs = []
    for qi in range(nq):
        qs = q_off + qi * tq
        k_last = min((lk - 1) // tk, (((qs + tq - 1) // CHUNK) * CHUNK + CHUNK - 1) // tk)
        pairs += [(qi, ki) for ki in range(k_last + 1)]
    qi_tab = jnp.asarray(np.array([p[0] for p in pairs], np.int32))
    ki_tab = jnp.asarray(np.array([p[1] for p in pairs], np.int32))
    kern = functools.partial(_attn_kernel, tq=tq, tk=tk, rq=rq, q_off=q_off, lk=lk,
                             n_heads=n_heads, out_scale=out_scale)
    grid_spec = pltpu.PrefetchScalarGridSpec(
        num_scalar_prefetch=2,
        grid=(bsz, n_heads, len(pairs)),
        in_specs=[pl.BlockSpec(memory_space=pltpu.SMEM),
                  pl.BlockSpec((1, tq, LANES), lambda b, h, s, qt, kt: (b, qt[s], h)),
                  pl.BlockSpec((1, tk, LANES), lambda b, h, s, qt, kt: (b, kt[s], h)),
                  pl.BlockSpec((1, tk, LANES), lambda b, h, s, qt, kt: (b, kt[s], h)),
                  pl.BlockSpec((1, LANES), lambda b, h, s, qt, kt: (0, 0))],
        out_specs=pl.BlockSpec((1, tq, LANES), lambda b, h, s, qt, kt: (b, qt[s], h)),
        scratch_shapes=[pltpu.VMEM((2, tq, LANES), F32), pltpu.VMEM((2, tq, LANES), F32),
                        pltpu.VMEM((2, tq, LANES), F32)])
    return pl.pallas_call(
        kern, grid_spec=grid_spec,
        out_shape=jax.ShapeDtypeStruct((bsz, lq, n_heads * LANES), BF16),
        compiler_params=_cparams("parallel", "parallel", "arbitrary"),
        name="diff_attention",
    )(qi_tab, ki_tab, par, q, k, v, sub_g.reshape(1, LANES))


def _ssm_kernel(z_ref, x_ref, b_ref, c_ref, dt_ref, dtb_ref, alog_ref, dsk_ref,
                cwx_ref, cwb_ref, cwc_ref, cbx_ref, cbb_ref, cbc_ref,
                px_ref, pb_ref, pc_ref, h0_ref, ng_ref,
                yg_ref, hfin_ref, ht_scr, xpx, xpb, xpc, *, lc, l_valid, n_conv):
    c = pl.program_id(2)
    nc = pl.num_programs(2)
    npair = ht_scr.shape[0]
    pad = SUBLANES
    nh = n_conv - 1

    @pl.when(c == 0)
    def _():
        for pr in range(npair):
            ht_scr[pr] = h0_ref[0, pr].T
        xpx[pad - nh:pad, :] = px_ref[0]
        xpb[pad - nh:pad, :] = pb_ref[0]
        xpc[pad - nh:pad, :] = pc_ref[0]

    @pl.when(c > 0)
    def _():
        xpx[pad - nh:pad, :] = xpx[pad + lc - nh:pad + lc, :]
        xpb[pad - nh:pad, :] = xpb[pad + lc - nh:pad + lc, :]
        xpc[pad - nh:pad, :] = xpc[pad + lc - nh:pad + lc, :]

    xpx[pad:pad + lc, :] = x_ref[0]
    xpb[pad:pad + lc, :] = b_ref[0]
    xpc[pad:pad + lc, :] = c_ref[0]

    def conv_silu(xp, cw_ref, cb_ref):
        acc = cb_ref[...]
        for w in range(n_conv):
            acc = acc + cw_ref[w:w + 1, :] * xp[pad - nh + w:pad - nh + w + lc, :]
        return acc * jax.nn.sigmoid(acc)

    xs = conv_silu(xpx, cwx_ref, cbx_ref)
    bm = conv_silu(xpb, cwb_ref, cbb_ref).astype(BF16)
    cm = conv_silu(xpc, cwc_ref, cbc_ref).astype(BF16)

    t_col = c * lc + lax.broadcasted_iota(jnp.int32, (lc, LANES), 0)
    dt_c = jax.nn.softplus(dt_ref[0] + dtb_ref[0])
    dt_c = jnp.where(t_col < l_valid, dt_c, 0.0)
    da_c = dt_c * (-jnp.exp(alog_ref[0]))
    ri = lax.broadcasted_iota(jnp.int32, (lc, lc), 0)
    ci = lax.broadcasted_iota(jnp.int32, (lc, lc), 1)
    causal = ri >= ci
    tril = jnp.where(causal, 1.0, 0.0).astype(BF16)
    acum_c = None
    for piece in _split3(da_c):
        t = jnp.dot(tril, piece, preferred_element_type=F32)
        acum_c = t if acum_c is None else acum_c + t
    acum_r = acum_c.T
    dt_r = dt_c.T
    a_last = acum_c[lc - 1:lc, :]
    wdec_c = jnp.exp(a_last - acum_c) * dt_c
    eac_c = jnp.exp(acum_c)
    ea_last = jnp.exp(a_last)

    cb = lax.dot_general(cm, bm, NT_DIMS, preferred_element_type=F32)
    lane = lax.broadcasted_iota(jnp.int32, (lc, LANES), 1)
    lane1 = lax.broadcasted_iota(jnp.int32, (1, LANES), 1)
    half = LANES // 2
    ys = []
    for pr in range(npair):
        xp_f = xs[:, pr * LANES:(pr + 1) * LANES]
        y = dsk_ref[:, pr * LANES:(pr + 1) * LANES] * xp_f
        for e in range(2):
            hh = 2 * pr + e
            seg = acum_c[:, hh:hh + 1] - acum_r[hh:hh + 1, :]
            lmat = jnp.exp(jnp.where(causal, seg, NEG_INF))
            m = (cb * lmat * dt_r[hh:hh + 1, :]).astype(BF16)
            sel = (lane < half) if e == 0 else (lane >= half)
            xh = jnp.where(sel, xp_f, 0.0).astype(BF16)
            y = y + jnp.dot(m, xh, preferred_element_type=F32)
        h_a, h_b = 2 * pr, 2 * pr + 1
        scale = jnp.where(lane < half, eac_c[:, h_a:h_a + 1], eac_c[:, h_b:h_b + 1])
        ht = ht_scr[pr]
        y = y + jnp.dot(cm, ht.astype(BF16), preferred_element_type=F32) * scale
        wd = jnp.where(lane < half, wdec_c[:, h_a:h_a + 1], wdec_c[:, h_b:h_b + 1])
        dec = jnp.where(lane1 < half, ea_last[:, h_a:h_a + 1], ea_last[:, h_b:h_b + 1])
        upd = lax.dot_general(bm, (wd * xp_f).astype(BF16), TN_DIMS, preferred_element_type=F32)
        ht_scr[pr] = dec * ht + upd
        ys.append(y)
    y = jnp.concatenate(ys, axis=-1)

    zz = z_ref[0]
    yg = y * (zz * jax.nn.sigmoid(zz))
    yg_ref[0] = (_rms(yg) * ng_ref[...]).astype(yg_ref.dtype)

    @pl.when(c == nc - 1)
    def _():
        for pr in range(npair):
            hfin_ref[0, pr] = ht_scr[pr].T


def ssm_branch(z, xbc, dt_pad, conv_past, h0, conv_w, conv_b, dtb_pad, alog_pad, dsk_exp, norm_g,
               *, lc, l_valid, n_groups, d_inner, d_state):
    bsz, length, _ = z.shape
    assert length % lc == 0 and lc % LANES == 0
    nc = length // lc
    gw = d_inner // n_groups
    assert gw % LANES == 0 and d_state == LANES
    npair = gw // LANES
    n_conv = conv_w.shape[0]
    xb0 = d_inner // d_state
    cb0 = xb0 + n_groups
    kern = functools.partial(_ssm_kernel, lc=lc, l_valid=l_valid, n_conv=n_conv)
    seq = lambda width, off: pl.BlockSpec((1, lc, width), lambda b, g, c: (b, c, off + g))
    par = lambda rows, width, off: pl.BlockSpec((rows, width), lambda b, g, c: (0, off + g))
    hist = lambda width, off: pl.BlockSpec((1, n_conv - 1, width), lambda b, g, c: (b, 0, off + g))
    state_spec = pl.BlockSpec((1, npair, LANES, d_state), lambda b, g, c: (b, g, 0, 0))
    yg, hfin = pl.pallas_call(
        kern,
        grid=(bsz, n_groups, nc),
        in_specs=[seq(gw, 0), seq(gw, 0), seq(d_state, xb0), seq(d_state, cb0), seq(LANES, 0),
                  pl.BlockSpec((1, 1, LANES), lambda b, g, c: (g, 0, 0)),
                  pl.BlockSpec((1, 1, LANES), lambda b, g, c: (g, 0, 0)),
                  par(1, gw, 0),
                  par(n_conv, gw, 0), par(n_conv, d_state, xb0), par(n_conv, d_state, cb0),
                  par(1, gw, 0), par(1, d_state, xb0), par(1, d_state, cb0),
                  hist(gw, 0), hist(d_state, xb0), hist(d_state, cb0),
                  state_spec, par(1, gw, 0)],
        out_specs=[seq(gw, 0), state_spec],
        out_shape=[jax.ShapeDtypeStruct((bsz, length, d_inner), BF16),
                   jax.ShapeDtypeStruct(h0.shape, F32)],
        scratch_shapes=[pltpu.VMEM((npair, d_state, LANES), F32),
                        pltpu.VMEM((lc + 2 * SUBLANES, gw), F32),
                        pltpu.VMEM((lc + 2 * SUBLANES, d_state), F32),
                        pltpu.VMEM((lc + 2 * SUBLANES, d_state), F32)],
        compiler_params=_cparams("parallel", "parallel", "arbitrary"),
        name="ssm_branch",
    )(z, xbc, xbc, xbc, dt_pad, dtb_pad, alog_pad, dsk_exp,
      conv_w, conv_w, conv_w, conv_b, conv_b, conv_b,
      conv_past, conv_past, conv_past, h0, norm_g)
    return yg, hfin


def _mix_kernel(x_ref, a_ref, y_ref, ga_ref, gs_ref, wa_ref, ws_ref, wo_ref, o_ref):
    o_a = jnp.dot(a_ref[...], wa_ref[...], preferred_element_type=F32)
    o_s = jnp.dot(y_ref[...], ws_ref[...], preferred_element_type=F32)
    mix = jax.nn.sigmoid(ga_ref[...]) * o_a + jax.nn.sigmoid(gs_ref[...]) * o_s
    o_ref[...] = x_ref[...] + jnp.dot(mix.astype(BF16), wo_ref[...], preferred_element_type=F32)


def mixer_out(x, attn, yg, gates, wa, ws, wo, tm=512):
    n, d = x.shape
    tm = min(tm, n)
    assert n % tm == 0 and gates.shape[1] == 2 * d
    row = lambda width, j=0: pl.BlockSpec((tm, width), lambda i, j=j: (i, j))
    full = lambda a: pl.BlockSpec(a.shape, lambda i: (0, 0))
    return pl.pallas_call(
        _mix_kernel,
        grid=(n // tm,),
        in_specs=[row(d), row(attn.shape[1]), row(yg.shape[1]), row(d, 0), row(d, 1),
                  full(wa), full(ws), full(wo)],
        out_specs=row(d),
        out_shape=jax.ShapeDtypeStruct((n, d), F32),
        compiler_params=_cparams("parallel"),
        name="mixer_out",
    )(x, attn, yg, gates, gates, wa, ws, wo)


def _staircase_pairs(k):
    return [(i, j) for i in range(k) for j in range(k) if (i + 1) * (j + 1) <= k]


def _oddeven_merge(lo, hi, r):
    step = r * 2
    if step < hi - lo:
        yield from _oddeven_merge(lo, hi, step)
        yield from _oddeven_merge(lo + r, hi, step)
        yield from [(i, i + r) for i in range(lo + r, hi - r, step)]
    else:
        yield (lo, lo + r)


def _oddeven_sort_pairs(lo, hi):
    if hi - lo >= 1:
        mid = lo + (hi - lo) // 2
        yield from _oddeven_sort_pairs(lo, mid)
        yield from _oddeven_sort_pairs(mid + 1, hi)
        yield from _oddeven_merge(lo, hi, 1)


def _sort_desc(vals):
    vals = list(vals)
    for i, j in _oddeven_sort_pairs(0, len(vals) - 1):
        vals[i], vals[j] = jnp.maximum(vals[i], vals[j]), jnp.minimum(vals[i], vals[j])
    return vals


def _bitonic_merge_desc(vals):
    vals = list(vals)
    n = len(vals)
    d = n // 2
    while d >= 1:
        for i in range(n):
            if i & d == 0:
                vals[i], vals[i + d] = jnp.maximum(vals[i], vals[i + d]), jnp.minimum(vals[i], vals[i + d])
        d //= 2
    return vals


def _merge_sublanes_top(vals):
    n = len(vals)
    shift = SUBLANES // 2
    while shift >= 1:
        other = [pltpu.roll(v, shift, 0) for v in vals]
        vals = _bitonic_merge_desc([jnp.maximum(vals[i], other[n - 1 - i]) for i in range(n)])
        shift //= 2
    return vals


def _router_kernel(x_ref, g_ref, wqt_ref, k1_ref, k2_ref,
                   h_ref, n1_ref, e1_ref, rk2_ref, e2_ref, qt_scr, *, n_heads, pairs):
    tt = x_ref.shape[0]
    nk = k1_ref.shape[0]
    ht = (_rms(x_ref[...]) * g_ref[...]).T.astype(BF16)
    h_ref[...] = ht
    qt_scr[...] = jnp.dot(wqt_ref[...], ht, preferred_element_type=F32).astype(BF16)
    sub = lax.broadcasted_iota(jnp.int32, (SUBLANES, LANES), 0)
    n_cv = -(-len(pairs) // SUBLANES)
    n_cv_pad = pl.next_power_of_2(n_cv)

    n_rows = nk // SUBLANES

    def head_tile(s1, s2, ls):
        s1cols = [s1[r * SUBLANES:(r + 1) * SUBLANES, ls] for r in range(n_rows)]
        s2cols = [s2[r * SUBLANES:(r + 1) * SUBLANES, ls] for r in range(n_rows)]
        v1 = _merge_sublanes_top(_sort_desc(s1cols))[:P_TOPK]
        v2 = _merge_sublanes_top(_sort_desc(s2cols))[:P_TOPK]
        sums = {(i, j): v1[i] + v2[j] for (i, j) in pairs}
        cvs = []
        for c in range(n_cv_pad):
            cv = jnp.full((SUBLANES, LANES), -jnp.inf, F32)
            for k in range(SUBLANES):
                pidx = c * SUBLANES + k
                if pidx < len(pairs):
                    cv = jnp.where(sub == k, sums[pairs[pidx]], cv)
            cvs.append(cv)
        top = _merge_candidates(_sort_desc(cvs))
        tau = top[P_TOPK - 1]
        zsum = jnp.zeros((SUBLANES, LANES), F32)
        for r in range(P_TOPK):
            zsum = zsum + jnp.exp(top[r] - top[0])
        cnts = []
        for i in range(P_TOPK):
            cnt = jnp.zeros((SUBLANES, LANES), F32)
            for (pi, pj) in pairs:
                if pi == i:
                    cnt = jnp.where(sums[(pi, pj)] >= tau, float(pj + 1), cnt)
            cnts.append(cnt)
        n1, e1, rk2, e2 = [], [], [], []
        for r in range(n_rows):
            nrow = jnp.zeros((SUBLANES, LANES), F32)
            rank = jnp.full((SUBLANES, LANES), float(P_TOPK), F32)
            for i in range(P_TOPK):
                nrow = jnp.where(s1cols[r] == v1[i], cnts[i], nrow)
            for j in reversed(range(P_TOPK)):
                rank = jnp.where(v2[j] <= s2cols[r], float(j), rank)
            n1.append(nrow)
            rk2.append(rank)
            e1.append(jnp.exp(s1cols[r] - v1[0]))
            e2.append(jnp.exp(s2cols[r] - v2[0]) / zsum)
        return n1, e1, rk2, e2

    def pack_pair(lo, hi):
        lo_w = pltpu.bitcast(lo.astype(BF16).astype(F32), jnp.uint32)
        hi_w = pltpu.bitcast(hi.astype(BF16).astype(F32), jnp.uint32)
        return hi_w | lax.shift_right_logical(lo_w, jnp.uint32(16))

    half = n_heads // 2

    def head_pair(hp, carry):
        scores = []
        for hd in (hp, hp + half):
            qh = qt_scr[pl.ds(pl.multiple_of(hd * nk, nk), nk), :]
            scores.append((jnp.dot(k1_ref[...], qh, preferred_element_type=F32),
                           jnp.dot(k2_ref[...], qh, preferred_element_type=F32)))
        for lg in range(tt // LANES):
            ls = slice(lg * LANES, (lg + 1) * LANES)
            lo = head_tile(scores[0][0], scores[0][1], ls)
            hi = head_tile(scores[1][0], scores[1][1], ls)
            for out_ref, lo_l, hi_l in zip((n1_ref, e1_ref, rk2_ref, e2_ref), lo, hi):
                for r in range(n_rows):
                    out_ref[0, hp, r * SUBLANES:(r + 1) * SUBLANES, ls] = pack_pair(lo_l[r], hi_l[r])
        return carry

    lax.fori_loop(0, half, head_pair, 0)


def _merge_candidates(cvs):
    assert len(cvs) == SUBLANES and P_TOPK == 2 * SUBLANES
    other = [pltpu.roll(v, SUBLANES // 2, 0) for v in cvs]
    vals = _bitonic_merge_desc(cvs + other[::-1])
    for shift in (SUBLANES // 4, SUBLANES // 8):
        oth = [pltpu.roll(v, shift, 0) for v in vals]
        vals = _bitonic_merge_desc([jnp.maximum(vals[i], oth[P_TOPK - 1 - i]) for i in range(P_TOPK)])
    return vals


def peer_router(x, g, wqt, k1p, k2p, *, n_heads, pairs, tt):
    n, d = x.shape
    assert n % tt == 0 and tt % LANES == 0
    nt = n // tt
    nk = k1p.shape[0]
    kern = functools.partial(_router_kernel, n_heads=n_heads, pairs=pairs)
    full = lambda a: pl.BlockSpec(a.shape, lambda i: (0,) * a.ndim)
    assert n_heads % 2 == 0
    tspec = pl.BlockSpec((1, n_heads // 2, nk, tt), lambda i: (i, 0, 0, 0))
    tshape = jax.ShapeDtypeStruct((nt, n_heads // 2, nk, tt), jnp.uint32)
    return pl.pallas_call(
        kern,
        grid=(nt,),
        in_specs=[pl.BlockSpec((tt, d), lambda i: (i, 0)), full(g), full(wqt), full(k1p), full(k2p)],
        out_specs=[pl.BlockSpec((d, tt), lambda i: (0, i)), tspec, tspec, tspec, tspec],
        out_shape=[jax.ShapeDtypeStruct((d, n), BF16), tshape, tshape, tshape, tshape],
        scratch_shapes=[pltpu.VMEM((wqt.shape[0], tt), BF16)],
        compiler_params=_cparams("parallel"),
        name="peer_router",
    )(x, g, wqt, k1p, k2p)


def _gelu_tanh(x):
    c = math.sqrt(2.0 / math.pi)
    return (0.5 * x) * (1.0 + jnp.tanh(x * (c + (c * 0.044715) * (x * x))))


def _experts_kernel(x_ref, h_ref, n1r_ref, e1r_ref, rk2_ref, e2_ref, u_ref, vt_ref,
                    o_ref, acc_scr, a_scr, *, rows_per_step, bh):
    s = pl.program_id(1)
    n_pairs, nk = rk2_ref.shape[1], rk2_ref.shape[2]
    tt = h_ref.shape[1]
    as_bf16 = lambda w: pltpu.bitcast(w, BF16)

    @pl.when(s == 0)
    def _():
        acc_scr[...] = jnp.zeros(acc_scr.shape, F32)

    pt = jnp.dot(u_ref[...], h_ref[...], preferred_element_type=F32)
    for a in range(rows_per_step):
        for lg in range(tt // LANES):
            ls = slice(lg * LANES, (lg + 1) * LANES)
            for b0 in range(0, nk, bh):
                gsum = jnp.zeros((2 * bh, LANES), BF16)
                for hp in range(n_pairs):
                    n1 = as_bf16(jnp.broadcast_to(n1r_ref[0, hp, a:a + 1, ls], (bh, LANES)))
                    e1 = as_bf16(jnp.broadcast_to(e1r_ref[0, hp, a:a + 1, ls], (bh, LANES)))
                    keep = as_bf16(rk2_ref[0, hp, b0:b0 + bh, ls]) < n1
                    w = e1 * as_bf16(e2_ref[0, hp, b0:b0 + bh, ls])
                    gsum = gsum + jnp.where(keep, w, jnp.zeros_like(w))
                gw = pltpu.bitcast(gsum, jnp.uint32)
                gate = (pltpu.bitcast(lax.shift_left(gw, jnp.uint32(16)), F32)
                        + pltpu.bitcast(gw & jnp.uint32(0xFFFF0000), F32))
                r0 = a * nk + b0
                a_scr[r0:r0 + bh, ls] = _gelu_tanh(pt[r0:r0 + bh, ls].astype(BF16)) * gate.astype(BF16)
    acc_scr[...] += jnp.dot(vt_ref[...], a_scr[...], preferred_element_type=F32)

    @pl.when(s == pl.num_programs(1) - 1)
    def _():
        o_ref[...] = x_ref[...] + acc_scr[...].T


def peer_experts(x, h, n1, e1, rk2, e2, u, vt):
    n, d = x.shape
    nt, n_pairs, nk, tt = rk2.shape
    n_exp = u.shape[0]
    rows_per_step = SUBLANES
    bh = 8 * SUBLANES
    eb = rows_per_step * nk
    assert n_exp % eb == 0 and n == nt * tt and nk % bh == 0
    kern = functools.partial(_experts_kernel, rows_per_step=rows_per_step, bh=bh)
    tspec = pl.BlockSpec((1, n_pairs, nk, tt), lambda i, s: (i, 0, 0, 0))
    rspec = pl.BlockSpec((1, n_pairs, rows_per_step, tt), lambda i, s: (i, 0, s, 0))
    return pl.pallas_call(
        kern,
        grid=(nt, n_exp // eb),
        in_specs=[pl.BlockSpec((tt, d), lambda i, s: (i, 0)),
                  pl.BlockSpec((d, tt), lambda i, s: (0, i)),
                  rspec, rspec, tspec, tspec,
                  pl.BlockSpec((eb, d), lambda i, s: (s, 0)),
                  pl.BlockSpec((d, eb), lambda i, s: (0, s))],
        out_specs=pl.BlockSpec((tt, d), lambda i, s: (i, 0)),
        out_shape=jax.ShapeDtypeStruct((n, d), F32),
        scratch_shapes=[pltpu.VMEM((d, tt), F32), pltpu.VMEM((eb, tt), BF16)],
        compiler_params=_cparams("parallel", "arbitrary"),
        name="peer_experts",
    )(x, h, n1, e1, rk2, e2, u, vt)


def _ple_kernel(x_ref, p_ref, g_ref, wg_ref, wp_ref, gf_ref, o_ref, *, final):
    x = x_ref[...]
    hg = (_rms(x) * g_ref[...]).astype(BF16)
    gate = jax.nn.sigmoid(jnp.dot(hg, wg_ref[...], preferred_element_type=F32))
    pe = jnp.dot(p_ref[...].astype(BF16), wp_ref[...], preferred_element_type=F32)
    x = x + gate * pe
    if final:
        x = _rms(x) * gf_ref[...]
    o_ref[...] = x


def ple_out(x, p, g, wg, wp, gf, *, final, tm=512):
    n, d = x.shape
    tm = min(tm, n)
    assert n % tm == 0
    full = lambda a: pl.BlockSpec(a.shape, lambda i: (0, 0))
    return pl.pallas_call(
        functools.partial(_ple_kernel, final=final),
        grid=(n // tm,),
        in_specs=[pl.BlockSpec((tm, d), lambda i: (i, 0)), pl.BlockSpec((tm, p.shape[1]), lambda i: (i, 0)),
                  full(g), full(wg), full(wp), full(gf)],
        out_specs=pl.BlockSpec((tm, d), lambda i: (i, 0)),
        out_shape=jax.ShapeDtypeStruct((n, d), F32),
        compiler_params=_cparams("parallel"),
        name="ple_out",
    )(x, p, g, wg, wp, gf)


def _round_up(x, m):
    return -(-x // m) * m


def _layer(x, p_i, k_past, v_past, conv_past, ssm_past, q_off, lw, cfg, lam_init, final_g):
    bsz, length, d = x.shape
    n = bsz * length
    n_ah, a_dk, a_dv = cfg["a_heads"], cfg["a_dk"], cfg["a_dv"]
    d_inner, n_groups, d_state = cfg["d_inner"], cfg["s_groups"], cfg["d_state"]
    x2 = x.reshape(n, d)

    hn = rms_cast(x2, lw["norm_mix_g"])
    (q,) = matmul(hn, lw["w_q"], [BF16])
    k_f32, k_bf = matmul(hn, lw["w_k"], [F32, BF16])
    v_f32, v_bf = matmul(hn, lw["w_v"], [F32, BF16])
    (z,) = matmul(hn, lw["w_z"], [F32])
    (xbc,) = matmul(hn, lw["w_xbc"], [F32])
    (dt_pad,) = matmul(hn, lw["w_dt"], [F32])
    (gates,) = matmul(hn, lw["w_gate"], [F32])

    hw = n_ah * LANES
    q3 = q.reshape(bsz, length, hw)
    k3 = k_bf.reshape(bsz, length, hw)
    v3 = v_bf.reshape(bsz, length, hw)
    if k_past is None:
        lk = length
        tq, tk = min(ATT_TQ, length), min(ATT_TK, length)
        rq = min(ATT_RQ, tq)
    else:
        past = k_past.shape[1]
        lk = past + length
        lk_pad = _round_up(lk, LANES)
        zpad = jnp.zeros((bsz, lk_pad - lk, hw), BF16)
        k3 = jnp.concatenate([k_past.reshape(bsz, past, hw).astype(BF16), k3, zpad], axis=1)
        v3 = jnp.concatenate([v_past.reshape(bsz, past, hw).astype(BF16), v3, zpad], axis=1)
        tq, tk, rq = length, lk_pad, length
    attn = diff_attention(q3, k3, v3, lw["attn_par"], lw["attn_subln_g"], lq=length, lk=lk, q_off=q_off,
                          tq=tq, tk=tk, rq=rq, n_heads=n_ah, out_scale=1.0 - lam_init)

    lc = min(SSD_CHUNK, _round_up(length, LANES))
    l_pad = _round_up(length, lc)
    z3 = z.reshape(bsz, length, d_inner)
    xbc3 = xbc.reshape(bsz, length, -1)
    dt3 = dt_pad.reshape(bsz, length, -1)
    if l_pad != length:
        padt = lambda a: jnp.pad(a, ((0, 0), (0, l_pad - length), (0, 0)))
        z3p, xbc3p, dt3p = padt(z3), padt(xbc3), padt(dt3)
    else:
        z3p, xbc3p, dt3p = z3, xbc3, dt3
    n_conv = lw["conv_w"].shape[0]
    if conv_past is None:
        conv_past = jnp.zeros((bsz, n_conv - 1, xbc3.shape[-1]), F32)
    if ssm_past is None:
        ssm_past = jnp.zeros((bsz, cfg["s_heads"], cfg["s_head_p"], d_state), F32)
    h0 = ssm_past.astype(F32).reshape(bsz, -1, LANES, d_state)
    yg, hfin = ssm_branch(z3p, xbc3p, dt3p, conv_past.astype(F32), h0, lw["conv_w"], lw["conv_b"],
                          lw["dtb_pad"], lw["alog_pad"], lw["dsk_exp"], lw["ssm_norm_g"],
                          lc=lc, l_valid=length, n_groups=n_groups, d_inner=d_inner, d_state=d_state)
    yg = yg[:, :length].reshape(n, d_inner)
    ssm_new = hfin.reshape(ssm_past.shape)
    if length >= n_conv - 1:
        conv_new = xbc3[:, length - (n_conv - 1):]
    else:
        conv_new = jnp.concatenate([conv_past.astype(F32), xbc3], axis=1)[:, -(n_conv - 1):]

    x2 = mixer_out(x2, attn.reshape(n, hw), yg, gates, lw["w_attn_o"], lw["w_ssm_o"], lw["w_out"])

    tt = min(PEER_TT, n)
    hp, n1, e1, rk2, e2 = peer_router(
        x2, lw["norm_ffn_g"], lw["peer_wqt"], lw["k1p"], lw["k2p"],
        n_heads=cfg["p_heads"], pairs=cfg["pairs"], tt=tt)
    x2 = peer_experts(x2, hp, n1, e1, rk2, e2, lw["peer_u"], lw["peer_vt"])

    x2 = ple_out(x2, p_i.reshape(n, -1), lw["norm_ple_g"], lw["ple_w_gate"], lw["ple_w_proj"],
                 lw["norm_ple_g"] if final_g is None else final_g, final=final_g is not None)

    k_new = k_f32.reshape(bsz, length, n_ah, 2, a_dk)
    v_new = v_f32.reshape(bsz, length, n_ah, a_dv)
    return x2.reshape(bsz, length, d), k_new, v_new, conv_new, ssm_new


def kernel(x_prompt, x_sample, p_prompt, p_sample, cache_k, cache_v, state_conv, state_ssm, norm_mix_g, w_in, lam_q1, lam_k1, lam_q2, lam_k2, attn_subln_g, w_attn_o, conv_w, conv_b, dt_bias, a_log, d_skip, ssm_norm_g, w_ssm_o, w_out, norm_ffn_g, peer_wq, peer_k1, peer_k2, peer_u, peer_v, norm_ple_g, ple_w_gate, ple_w_proj, final_norm_g):
    depth = w_in.shape[0]
    d = x_prompt.shape[-1]
    a_heads, a_dk, a_dv = cache_k.shape[3], cache_k.shape[5], cache_v.shape[4]
    s_heads, s_head_p, d_state = state_ssm.shape[2:]
    d_inner = ssm_norm_g.shape[1]
    conv_dim = conv_w.shape[2]
    s_groups = (conv_dim - d_inner) // (2 * d_state)
    hpg = s_heads // s_groups
    p_nkeys, p_half = peer_k1.shape[1], peer_k1.shape[2]
    p_heads = peer_wq.shape[2] // (2 * p_half)
    assert 2 * a_dk == LANES and a_dv == LANES and 2 * s_head_p == LANES and d_state == LANES
    assert p_nkeys == LANES and 2 * p_half == LANES and hpg <= LANES
    pairs = _staircase_pairs(P_TOPK)
    assert len(pairs) <= SUBLANES * SUBLANES
    cfg = dict(a_heads=a_heads, a_dk=a_dk, a_dv=a_dv, d_inner=d_inner, s_groups=s_groups, d_state=d_state,
               s_heads=s_heads, s_head_p=s_head_p, p_heads=p_heads, pairs=pairs)

    c_q = a_heads * 2 * a_dk
    c_v = a_heads * a_dv
    splits = np.cumsum([0, c_q, c_q, c_v, d_inner, conv_dim, s_heads, 2 * d])
    row = lambda a: a.astype(F32).reshape(1, -1)

    layers = []
    for i in range(depth):
        w = w_in[i]
        sl = lambda j: w[:, splits[j]:splits[j + 1]]
        w_dt = jnp.zeros((d, s_groups, LANES), F32).at[:, :, :hpg].set(sl(5).reshape(d, s_groups, hpg))
        grp = lambda a, fill: jnp.full((s_groups, 1, LANES), fill, F32).at[:, 0, :hpg].set(
            a.astype(F32).reshape(s_groups, hpg))
        f32 = jnp.float32
        lam_init = 0.8 - 0.6 * math.exp(-0.3 * i)
        lam = (jnp.exp(jnp.sum(lam_q1[i].astype(f32) * lam_k1[i].astype(f32)))
               - jnp.exp(jnp.sum(lam_q2[i].astype(f32) * lam_k2[i].astype(f32))) + lam_init)
        slopes = 2.0 ** (-8.0 * (jnp.arange(a_heads, dtype=f32) + 1.0) / a_heads)
        zeros_half = jnp.zeros((p_nkeys, p_half), F32)
        lw = dict(
            norm_mix_g=norm_mix_g[i],
            w_q=(sl(0) * (a_dk ** -0.5)).astype(BF16), w_k=sl(1).astype(BF16), w_v=sl(2).astype(BF16),
            w_z=sl(3).astype(BF16), w_xbc=sl(4).astype(BF16), w_gate=sl(6).astype(BF16),
            w_dt=w_dt.reshape(d, s_groups * LANES).astype(BF16),
            attn_par=jnp.concatenate([slopes, lam.reshape(1)]).astype(F32),
            attn_subln_g=attn_subln_g[i].astype(F32),
            w_attn_o=w_attn_o[i].astype(BF16), w_ssm_o=w_ssm_o[i].astype(BF16), w_out=w_out[i].astype(BF16),
            conv_w=conv_w[i].astype(F32), conv_b=row(conv_b[i]),
            dtb_pad=grp(dt_bias[i], 0.0), alog_pad=grp(a_log[i], 0.0),
            dsk_exp=row(jnp.repeat(d_skip[i], s_head_p)), ssm_norm_g=row(ssm_norm_g[i]),
            norm_ffn_g=row(norm_ffn_g[i]), peer_wqt=peer_wq[i].astype(BF16).T,
            k1p=jnp.concatenate([peer_k1[i], zeros_half], axis=1).astype(BF16),
            k2p=jnp.concatenate([zeros_half, peer_k2[i]], axis=1).astype(BF16),
            peer_u=peer_u[i].astype(BF16), peer_vt=peer_v[i].astype(BF16).T,
            norm_ple_g=row(norm_ple_g[i]), ple_w_gate=ple_w_gate[i].astype(BF16),
            ple_w_proj=ple_w_proj[i].astype(BF16),
        )
        layers.append((lw, lam_init))

    def run_group(x, p, ck, cv, cconv, cssm, q_off):
        k_rows, v_rows, conv_rows, ssm_rows = [], [], [], []
        for i, (lw, lam_init) in enumerate(layers):
            last = i == depth - 1
            x, k_new, v_new, conv_new, ssm_new = _layer(
                x, p[i],
                None if ck is None else ck[i], None if cv is None else cv[i],
                None if cconv is None else cconv[i], None if cssm is None else cssm[i],
                q_off, lw, cfg, lam_init, row(final_norm_g) if last else None)
            k_rows.append(k_new)
            v_rows.append(v_new)
            conv_rows.append(conv_new)
            ssm_rows.append(ssm_new)
        return x, jnp.stack(k_rows), jnp.stack(v_rows), jnp.stack(conv_rows), jnp.stack(ssm_rows)

    y_p, k_p, v_p, conv_p, ssm_p = run_group(x_prompt, p_prompt, None, None, None, None, 0)
    y_s, k_s, v_s, conv_s, ssm_s = run_group(x_sample, p_sample, cache_k, cache_v, state_conv, state_ssm,
                                             cache_k.shape[2])
    return (y_p, y_s, k_p, v_p, conv_p, ssm_p, k_s, v_s, conv_s, ssm_s)
```
